```python
import math
import jax, jax.numpy as jnp
from jax import lax
import numpy as np

D_MODEL = 4096
BATCH = 1
SEQ = 8192
DEPTH = 2

GRID_W = 64
CTX_LEN = 256
N_MIXERS = 2
N_DIFF_LAYERS = (DEPTH + N_MIXERS - 1) // N_MIXERS
N_RET_LAYERS = DEPTH // N_MIXERS
DIFF_HEADS = 16
DIFF_HEAD_DIM = D_MODEL // (2 * DIFF_HEADS)
DIFF_V_DIM = 2 * DIFF_HEAD_DIM
Q_BLOCK = 128
ROPE_BASE = 10000.0
RET_HEADS = 16
RET_QK_DIM = D_MODEL // RET_HEADS
RET_V_DIM = 2 * RET_QK_DIM
RET_CHUNK = 128
N_EXPERTS = 16
EXPERT_FF = D_MODEL // 4
EC_CAPACITY_FACTOR = 2
NORM_EPS = 1e-6

kernel_name = "hybrid_diffattn_retention_ecmoe_dit"


def rmsnorm(x, gain=None, eps=NORM_EPS):
    xf = x.astype(jnp.float32)
    y = xf * lax.rsqrt(jnp.mean(xf * xf, axis=-1, keepdims=True) + eps)
    if gain is not None:
        y = y * gain.astype(jnp.float32)
    return y.astype(x.dtype)


def adaln_modulation(cvec, w, b):
    return jnp.split(jax.nn.silu(cvec) @ w + b, 6, axis=-1)


def grid_rope_angles(T, rot_dim):
    rows = T // GRID_W
    row = jnp.repeat(jnp.arange(rows), GRID_W).astype(jnp.float32)
    col = (jnp.arange(T) % GRID_W).astype(jnp.float32)
    axis_dim = rot_dim // 2
    inv_freq = ROPE_BASE ** (-jnp.arange(0, axis_dim, 2, dtype=jnp.float32) / axis_dim)
    return row[:, None] * inv_freq, col[:, None] * inv_freq


def rope_rotate(x, ang):
    n = ang.shape[-1]
    cos = jnp.cos(ang)[:, None, :].astype(x.dtype)
    sin = jnp.sin(ang)[:, None, :].astype(x.dtype)
    x1, x2 = x[..., :n], x[..., n:]
    return jnp.concatenate([x1 * cos - x2 * sin, x2 * cos + x1 * sin], axis=-1)


def apply_axial_rope(x, angles):
    ang_row, ang_col = angles
    half = x.shape[-1] // 2
    return jnp.concatenate([rope_rotate(x[..., :half], ang_row),
                            rope_rotate(x[..., half:], ang_col)], axis=-1)


def heads_first(a):
    return jnp.swapaxes(a, 1, 2)


def merge_heads(o):
    B, H, T, d = o.shape
    return jnp.swapaxes(o, 1, 2).reshape(B, T, H * d)


def diff_softmax_mix(q1, q2, k1, k2, v, lam):
    scale = DIFF_HEAD_DIM ** -0.5
    s1 = jnp.einsum('bhqd,bhkd->bhqk', q1, k1).astype(jnp.float32) * scale
    s2 = jnp.einsum('bhqd,bhkd->bhqk', q2, k2).astype(jnp.float32) * scale
    a = jax.nn.softmax(s1, axis=-1) - lam * jax.nn.softmax(s2, axis=-1)
    return jnp.einsum('bhqk,bhkv->bhqv', a.astype(v.dtype), v)


def diff_attention(h_ctx, h_lat, w_in, w_out, q_gain, k_gain, lq1, lk1, lq2, lk2, subln_gain,
                   lambda_init, angles, need_ctx_out):
    def project(h, use_rope):
        B, T, _ = h.shape
        q, k, v = jnp.split(h @ w_in, 3, axis=-1)
        q = rmsnorm(q.reshape(B, T, DIFF_HEADS, 2, DIFF_HEAD_DIM), q_gain)
        k = rmsnorm(k.reshape(B, T, DIFF_HEADS, 2, DIFF_HEAD_DIM), k_gain)
        q1, q2, k1, k2 = q[..., 0, :], q[..., 1, :], k[..., 0, :], k[..., 1, :]
        if use_rope:
            q1, q2, k1, k2 = [apply_axial_rope(a, angles) for a in (q1, q2, k1, k2)]
        v = v.reshape(B, T, DIFF_HEADS, DIFF_V_DIM)
        return [heads_first(a) for a in (q1, q2, k1, k2, v)]

    q1c, q2c, k1c, k2c, vc = project(h_ctx, False)
    q1l, q2l, k1l, k2l, vl = project(h_lat, True)
    lam = (jnp.exp(jnp.sum(lq1 * lk1)) - jnp.exp(jnp.sum(lq2 * lk2))).astype(jnp.float32) + lambda_init

    def finish(o):
        o = rmsnorm(o, subln_gain) * (1.0 - lambda_init)
        return merge_heads(o) @ w_out

    ctx_out = finish(diff_softmax_mix(q1c, q2c, k1c, k2c, vc, lam)) if need_ctx_out else None

    k1a = jnp.concatenate([k1c, k1l], axis=2)
    k2a = jnp.concatenate([k2c, k2l], axis=2)
    va = jnp.concatenate([vc, vl], axis=2)
    B, H, T, d = q1l.shape
    nb = T // Q_BLOCK

    def blocks(a):
        return jnp.moveaxis(a.reshape(B, H, nb, Q_BLOCK, a.shape[-1]), 2, 0)

    o_blocks = lax.map(lambda qs: diff_softmax_mix(qs[0], qs[1], k1a, k2a, va, lam),
                       (blocks(q1l), blocks(q2l)))
    o_lat = jnp.moveaxis(o_blocks, 0, 2).reshape(B, H, T, DIFF_V_DIM)
    return ctx_out, finish(o_lat)


def retention_scan(q, k, v, log_gamma, s0):
    B, H, T, dk = q.shape
    dv = v.shape[-1]
    C = RET_CHUNK
    n = T // C
    q, k, v = q.astype(jnp.float32), k.astype(jnp.float32), v.astype(jnp.float32)
    pos = jnp.arange(C, dtype=jnp.float32)
    lg = log_gamma.astype(jnp.float32)
    rel = pos[:, None] - pos[None, :]
    intra = jnp.where(rel >= 0, jnp.exp(lg[:, None, None] * jnp.maximum(rel, 0.0)), 0.0)
    q_decay = jnp.exp(lg[:, None] * (pos + 1.0))[..., None]
    k_decay = jnp.exp(lg[:, None] * (C - 1.0 - pos))[..., None]
    chunk_decay = jnp.exp(lg * C)[:, None, None]

    def chunks(a):
        return jnp.moveaxis(a.reshape(B, H, n, C, a.shape[-1]), 2, 0)

    def step(S, xs):
        qc, kc, vc = xs
        scores = jnp.einsum('bhid,bhjd->bhij', qc, kc) * intra
        o = (jnp.einsum('bhij,bhjv->bhiv', scores, vc)
             + jnp.einsum('bhid,bhdv->bhiv', qc * q_decay, S))
        S = S * chunk_decay + jnp.einsum('bhjd,bhjv->bhdv', kc * k_decay, vc)
        return S, o

    S, o = lax.scan(step, s0.astype(jnp.float32), (chunks(q), chunks(k), chunks(v)))
    o = jnp.moveaxis(o, 0, 2).reshape(B, H, T, dv)
    return o, S


def retention(h_ctx, h_lat, w_in, w_out, decay_fwd, decay_bwd, angles, need_ctx_out):
    D = D_MODEL

    def project(h, use_rope):
        B, T, _ = h.shape
        q, k, v, g_f, g_b = jnp.split(h @ w_in, [D, 2 * D, 4 * D, 6 * D], axis=-1)
        q = q.reshape(B, T, RET_HEADS, RET_QK_DIM)
        k = k.reshape(B, T, RET_HEADS, RET_QK_DIM)
        if use_rope:
            q, k = apply_axial_rope(q, angles), apply_axial_rope(k, angles)
        k = k * (RET_QK_DIM ** -0.5)
        v = v.reshape(B, T, RET_HEADS, RET_V_DIM)
        return heads_first(q), heads_first(k), heads_first(v), g_f, g_b

    lg_f = jnp.log1p(-jnp.exp(decay_fwd.astype(jnp.float32)))
    lg_b = jnp.log1p(-jnp.exp(decay_bwd.astype(jnp.float32)))

    qc, kc, vc, gfc, gbc = project(h_ctx, False)
    ql, kl, vl, gfl, gbl = project(h_lat, True)
    B = h_lat.shape[0]
    zeros = jnp.zeros((B, RET_HEADS, RET_QK_DIM, RET_V_DIM), jnp.float32)
    flip = lambda a: jnp.flip(a, axis=2)

    oc_f, s_f = retention_scan(qc, kc, vc, lg_f, zeros)
    ol_f, _ = retention_scan(ql, kl, vl, lg_f, s_f)
    oc_b, s_b = retention_scan(flip(qc), flip(kc), flip(vc), lg_b, zeros)
    ol_b, _ = retention_scan(flip(ql), flip(kl), flip(vl), lg_b, s_b)
    oc_b, ol_b = flip(oc_b), flip(ol_b)

    def finish(o_f, o_b, g_f, g_b):
        y_f = merge_heads(rmsnorm(o_f)).astype(g_f.dtype)
        y_b = merge_heads(rmsnorm(o_b)).astype(g_b.dtype)
        return (jax.nn.silu(g_f) * y_f + jax.nn.silu(g_b) * y_b) @ w_out

    ctx_out = finish(oc_f, oc_b, gfc, gbc) if need_ctx_out else None
    return ctx_out, finish(ol_f, ol_b, gfl, gbl)


def expert_choice_ffn(h, w_router, w_gate, w_up, w_down):
    B, T, _ = h.shape
    capacity = EC_CAPACITY_FACTOR * T // N_EXPERTS
    affinity = jax.nn.softmax(jnp.einsum('btd,de->bte', h, w_router).astype(jnp.float32), axis=-1)
    gate, idx = lax.top_k(jnp.swapaxes(affinity, 1, 2), capacity)
    xg = jax.vmap(lambda hb, ib: hb[ib])(h, idx)
    hidden = (jax.nn.silu(jnp.einsum('becd,edf->becf', xg, w_gate))
              * jnp.einsum('becd,edf->becf', xg, w_up))
    y = jnp.einsum('becf,efd->becd', hidden, w_down) * gate[..., None].astype(h.dtype)
    return jax.vmap(lambda yb, ib: jnp.zeros((T, D_MODEL), yb.dtype)
                    .at[ib.reshape(-1)].add(yb.reshape(-1, D_MODEL)))(y, idx)


def setup_inputs(seed: int = 0) -> dict:
    key = jax.random.key(seed)
    ks = jax.random.split(key, 25)
    f32 = jnp.float32
    D = D_MODEL

    def nrm(k, shape, scale):
        return jax.random.normal(k, shape, f32) * scale

    base_decay = -(5.0 + jnp.arange(RET_HEADS, dtype=f32)) * math.log(2.0)
    return {
        "x": nrm(ks[0], (BATCH, SEQ, D), 1.0),
        "c": nrm(ks[1], (BATCH, D), 1.0),
        "ctx": nrm(ks[2], (BATCH, CTX_LEN, D), 1.0),
        "c_ctx": nrm(ks[3], (D,), 1.0),
        "ada_w": nrm(ks[4], (DEPTH, D, 6 * D), 0.5 * D ** -0.5),
        "ada_b": nrm(ks[5], (DEPTH, 6 * D), 0.01),
        "norm_mix": 1.0 + nrm(ks[6], (DEPTH, D), 0.02),
        "norm_ffn": 1.0 + nrm(ks[7], (DEPTH, D), 0.02),
        "diff_w_in": nrm(ks[8], (N_DIFF_LAYERS, D, 3 * D), D ** -0.5),
        "diff_w_out": nrm(ks[9], (N_DIFF_LAYERS, D, D), D ** -0.5),
        "diff_q_norm": 1.0 + nrm(ks[10], (N_DIFF_LAYERS, DIFF_HEAD_DIM), 0.02),
        "diff_k_norm": 1.0 + nrm(ks[11], (N_DIFF_LAYERS, DIFF_HEAD_DIM), 0.02),
        "diff_lambda_q1": nrm(ks[12], (N_DIFF_LAYERS, DIFF_HEAD_DIM), 0.1),
        "diff_lambda_k1": nrm(ks[13], (N_DIFF_LAYERS, DIFF_HEAD_DIM), 0.1),
        "diff_lambda_q2": nrm(ks[14], (N_DIFF_LAYERS, DIFF_HEAD_DIM), 0.1),
        "diff_lambda_k2": nrm(ks[15], (N_DIFF_LAYERS, DIFF_HEAD_DIM), 0.1),
        "diff_subln": 1.0 + nrm(ks[16], (N_DIFF_LAYERS, DIFF_V_DIM), 0.02),
        "ret_w_in": nrm(ks[17], (N_RET_LAYERS, D, 8 * D), D ** -0.5),
        "ret_w_out": nrm(ks[18], (N_RET_LAYERS, 2 * D, D), (2 * D) ** -0.5),
        "ret_decay_fwd": base_decay + nrm(ks[19], (N_RET_LAYERS, RET_HEADS), 0.01),
        "ret_decay_bwd": base_decay + nrm(ks[20], (N_RET_LAYERS, RET_HEADS), 0.01),
        "router_w": nrm(ks[21], (DEPTH, D, N_EXPERTS), D ** -0.5),
        "expert_w_gate": nrm(ks[22], (DEPTH, N_EXPERTS, D, EXPERT_FF), D ** -0.5),
        "expert_w_up": nrm(ks[23], (DEPTH, N_EXPERTS, D, EXPERT_FF), D ** -0.5),
        "expert_w_down": nrm(ks[24], (DEPTH, N_EXPERTS, EXPERT_FF, D), EXPERT_FF ** -0.5),
    }


def reference(x, c, ctx, c_ctx, ada_w, ada_b, norm_mix, norm_ffn, diff_w_in, diff_w_out,
              diff_q_norm, diff_k_norm, diff_lambda_q1, diff_lambda_k1, diff_lambda_q2,
              diff_lambda_k2, diff_subln, ret_w_in, ret_w_out, ret_decay_fwd, ret_decay_bwd,
              router_w, expert_w_gate, expert_w_up, expert_w_down):
    T = x.shape[1]
    diff_angles = grid_rope_angles(T, DIFF_HEAD_DIM)
    ret_angles = grid_rope_angles(T, RET_QK_DIM)
    h_lat, h_ctx = x, ctx
    for i in range(DEPTH):
        last = i == DEPTH - 1
        sh1_l, sc1_l, g1_l, sh2_l, sc2_l, g2_l = [m[:, None, :] for m in adaln_modulation(c, ada_w[i], ada_b[i])]
        sh1_c, sc1_c, g1_c, sh2_c, sc2_c, g2_c = adaln_modulation(c_ctx, ada_w[i], ada_b[i])
        n_lat = rmsnorm(h_lat, norm_mix[i]) * (1.0 + sc1_l) + sh1_l
        n_ctx = rmsnorm(h_ctx, norm_mix[i]) * (1.0 + sc1_c) + sh1_c
        j = i // N_MIXERS
        if i % N_MIXERS == 0:
            mix_ctx, mix_lat = diff_attention(
                n_ctx, n_lat, diff_w_in[j], diff_w_out[j], diff_q_norm[j], diff_k_norm[j],
                diff_lambda_q1[j], diff_lambda_k1[j], diff_lambda_q2[j], diff_lambda_k2[j],
                diff_subln[j], 0.8 - 0.6 * math.exp(-0.3 * i), diff_angles, not last)
        else:
            mix_ctx, mix_lat = retention(
                n_ctx, n_lat, ret_w_in[j], ret_w_out[j], ret_decay_fwd[j], ret_decay_bwd[j],
                ret_angles, not last)
        h_lat = h_lat + g1_l * mix_lat
        n_lat = rmsnorm(h_lat, norm_ffn[i]) * (1.0 + sc2_l) + sh2_l
        h_lat = h_lat + g2_l * expert_choice_ffn(n_lat, router_w[i], expert_w_gate[i],
                                                 expert_w_up[i], expert_w_down[i])
        if not last:
            h_ctx = h_ctx + g1_c * mix_ctx
            n_ctx = rmsnorm(h_ctx, norm_ffn[i]) * (1.0 + sc2_c) + sh2_c
            h_ctx = h_ctx + g2_c * expert_choice_ffn(n_ctx, router_w[i], expert_w_gate[i],
                                                     expert_w_up[i], expert_w_down[i])
    return h_lat
```

```python
import functools
import math

import jax
import jax.numpy as jnp
from jax import lax
from jax.experimental import pallas as pl
from jax.experimental.pallas import tpu as pltpu

F32 = jnp.float32
BF16 = jnp.bfloat16
I32 = jnp.int32

NORM_EPS = 1e-6
GRID_W = 64
ROPE_BASE = 10000.0
DIFF_HEAD_DIM = 128
DIFF_V_DIM = 2 * DIFF_HEAD_DIM
RET_QK_DIM = 256
RET_V_DIM = 2 * RET_QK_DIM
RET_CHUNK = 256
EC_CAPACITY_FACTOR = 2
ROW_TILE = 256
LANES = 128
V7X_VMEM_BYTES = 64 * 1024 * 1024
VMEM_LIMIT = V7X_VMEM_BYTES - 8 * 1024 * 1024


def _params(*sem):
    return pltpu.CompilerParams(dimension_semantics=sem, vmem_limit_bytes=VMEM_LIMIT)


def _silu(x):
    return x * (1.0 / (1.0 + jnp.exp(-x)))


def _rms(x):
    return x * lax.rsqrt(jnp.mean(x * x, axis=-1, keepdims=True) + NORM_EPS)


def _largest_divisor(n, candidates):
    for cand in candidates:
        if n % cand == 0:
            return cand
    raise ValueError(f"no tile in {candidates} divides {n}")


def _mm_tiles(m, k):
    bn = 512 if k <= 4096 else 256
    bm = _largest_divisor(m, (1024, 768, 512, 256) if k <= 4096 else (512, 256))
    return bm, bn


def _adaln_kernel(c_ref, w_ref, b_ref, o_ref):
    s = _silu(c_ref[...])
    o_ref[...] = jnp.dot(s.astype(BF16), w_ref[...].astype(BF16), preferred_element_type=F32) + b_ref[...]


def adaln(cvec, w, b):
    d, n = w.shape
    bn = 512
    return pl.pallas_call(
        _adaln_kernel,
        grid=(n // bn,),
        in_specs=[pl.BlockSpec((8, d), lambda j: (0, 0)),
                  pl.BlockSpec((d, bn), lambda j: (0, j)),
                  pl.BlockSpec((1, bn), lambda j: (0, j))],
        out_specs=pl.BlockSpec((8, bn), lambda j: (0, j)),
        out_shape=jax.ShapeDtypeStruct((8, n), F32),
        compiler_params=_params("arbitrary"),
        name="adaln",
    )(cvec, w, b.reshape(1, n))


def _mod_rows(mod_ref, is_ctx):
    return jnp.where(is_ctx, mod_ref[1:2, :], mod_ref[0:1, :])


def _norm_mod_kernel(h_ref, gain_ref, sh_ref, sc_ref, o_ref, *, n_lat_tiles):
    is_ctx = pl.program_id(0) >= n_lat_tiles
    y = _rms(h_ref[...]) * gain_ref[...]
    o_ref[...] = (y * (1.0 + _mod_rows(sc_ref, is_ctx)) + _mod_rows(sh_ref, is_ctx)).astype(o_ref.dtype)


def norm_mod(h, gain, mod, k_shift, k_scale, t_lat):
    rows, d = h.shape
    return pl.pallas_call(
        functools.partial(_norm_mod_kernel, n_lat_tiles=t_lat // ROW_TILE),
        grid=(rows // ROW_TILE,),
        in_specs=[pl.BlockSpec((ROW_TILE, d), lambda i: (i, 0)),
                  pl.BlockSpec((1, d), lambda i: (0, 0)),
                  pl.BlockSpec((8, d), lambda i: (0, k_shift)),
                  pl.BlockSpec((8, d), lambda i: (0, k_scale))],
        out_specs=pl.BlockSpec((ROW_TILE, d), lambda i: (i, 0)),
        out_shape=jax.ShapeDtypeStruct((rows, d), BF16),
        compiler_params=_params("arbitrary"),
        name="norm_mod",
    )(h, gain.reshape(1, d), mod, mod)


def _mm_acc(a_ref, w_ref, wbf_ref):
    @pl.when(pl.program_id(1) == 0)
    def _():
        wbf_ref[...] = w_ref[...].astype(BF16)

    return jnp.dot(a_ref[...], wbf_ref[...], preferred_element_type=F32)


def _rot_half_64(x):
    lane = lax.broadcasted_iota(I32, (1, LANES), 1)
    first = (lane % 64) < 32
    return jnp.where(first, pltpu.roll(x, 96, 1), pltpu.roll(x, 32, 1))


def _mm_diff_in_kernel(a_ref, w_ref, qg_ref, kg_ref, cos_ref, sin_ref, o_ref, wbf_ref, *, nq, nqk, qscale):
    j = pl.program_id(0)
    acc = _mm_acc(a_ref, w_ref, wbf_ref)

    @pl.when(j < nqk)
    def _():
        is_q = j < nq
        gain = jnp.where(is_q, qg_ref[...], kg_ref[...])
        scale = jnp.where(is_q, qscale, 1.0)
        cos, sin = cos_ref[...], sin_ref[...]
        for cidx in range(acc.shape[1] // LANES):
            sl = slice(cidx * LANES, (cidx + 1) * LANES)
            y = _rms(acc[:, sl]) * gain
            y = y * cos + _rot_half_64(y) * sin
            o_ref[:, sl] = (y * scale).astype(o_ref.dtype)

    @pl.when(j >= nqk)
    def _():
        o_ref[...] = acc.astype(o_ref.dtype)


def mm_diff_in(a, w, q_gain, k_gain, cos, sin):
    m, k = a.shape
    n = w.shape[1]
    d = n // 3
    bm, bn = _mm_tiles(m, k)
    kern = functools.partial(_mm_diff_in_kernel, nq=d // bn, nqk=2 * d // bn, qscale=DIFF_HEAD_DIM ** -0.5)
    return pl.pallas_call(
        kern,
        grid=(n // bn, m // bm),
        in_specs=[pl.BlockSpec((bm, k), lambda j, i: (i, 0)),
                  pl.BlockSpec((k, bn), lambda j, i: (0, j)),
                  pl.BlockSpec((1, LANES), lambda j, i: (0, 0)),
                  pl.BlockSpec((1, LANES), lambda j, i: (0, 0)),
                  pl.BlockSpec((bm, LANES), lambda j, i: (i, 0)),
                  pl.BlockSpec((bm, LANES), lambda j, i: (i, 0))],
        out_specs=pl.BlockSpec((bm, bn), lambda j, i: (i, j)),
        out_shape=jax.ShapeDtypeStruct((m, n), BF16),
        scratch_shapes=[pltpu.VMEM((k, bn), BF16)],
        compiler_params=_params("arbitrary", "arbitrary"),
        name="mm_diff_in",
    )(a, w, q_gain.reshape(1, LANES), k_gain.reshape(1, LANES), cos, sin)


def _mm_ret_in_kernel(a_ref, w_ref, cos_ref, sin_ref, o_ref, wbf_ref, *, nq, nqk, kscale):
    j = pl.program_id(0)
    acc = _mm_acc(a_ref, w_ref, wbf_ref)

    @pl.when(j < nqk)
    def _():
        scale = jnp.where(j < nq, 1.0, kscale)
        for cidx in range(acc.shape[1] // LANES):
            sl = slice(cidx * LANES, (cidx + 1) * LANES)
            tl = slice((cidx % 2) * LANES, (cidx % 2 + 1) * LANES)
            x = acc[:, sl]
            y = x * cos_ref[:, tl] + pltpu.roll(x, 64, 1) * sin_ref[:, tl]
            o_ref[:, sl] = (y * scale).astype(o_ref.dtype)

    @pl.when(j >= nqk)
    def _():
        o_ref[...] = acc.astype(o_ref.dtype)


def mm_ret_in(a, w, cos, sin):
    m, k = a.shape
    n = w.shape[1]
    d = n // 8
    bm, bn = _mm_tiles(m, k)
    kern = functools.partial(_mm_ret_in_kernel, nq=d // bn, nqk=2 * d // bn, kscale=RET_QK_DIM ** -0.5)
    return pl.pallas_call(
        kern,
        grid=(n // bn, m // bm),
        in_specs=[pl.BlockSpec((bm, k), lambda j, i: (i, 0)),
                  pl.BlockSpec((k, bn), lambda j, i: (0, j)),
                  pl.BlockSpec((bm, RET_QK_DIM), lambda j, i: (i, 0)),
                  pl.BlockSpec((bm, RET_QK_DIM), lambda j, i: (i, 0))],
        out_specs=pl.BlockSpec((bm, bn), lambda j, i: (i, j)),
        out_shape=jax.ShapeDtypeStruct((m, n), BF16),
        scratch_shapes=[pltpu.VMEM((k, bn), BF16)],
        compiler_params=_params("arbitrary", "arbitrary"),
        name="mm_ret_in",
    )(a, w, cos, sin)


def _mm_resid_kernel(a_ref, w_ref, h_ref, g_ref, o_ref, wbf_ref, *, t_lat):
    acc = _mm_acc(a_ref, w_ref, wbf_ref)
    bm = acc.shape[0]
    row = pl.program_id(1) * bm + lax.broadcasted_iota(I32, (bm, 1), 0)
    gate = jnp.where(row >= t_lat, g_ref[1:2, :], g_ref[0:1, :])
    o_ref[...] = h_ref[...] + gate * acc


def mm_resid(a, w, h, mod, k_gate, t_lat, single_buffer_w=False):
    m, k = a.shape
    n = w.shape[1]
    bm, bn = _mm_tiles(m, k)
    if single_buffer_w:
        bn = 512
        w_spec = pl.BlockSpec((k, bn), lambda j, i: (0, j), pipeline_mode=pl.Buffered(1))
    else:
        w_spec = pl.BlockSpec((k, bn), lambda j, i: (0, j))
    nb = n // bn
    return pl.pallas_call(
        functools.partial(_mm_resid_kernel, t_lat=t_lat),
        grid=(nb, m // bm),
        in_specs=[pl.BlockSpec((bm, k), lambda j, i: (i, 0)),
                  w_spec,
                  pl.BlockSpec((bm, bn), lambda j, i: (i, j)),
                  pl.BlockSpec((8, bn), lambda j, i: (0, k_gate * nb + j))],
        out_specs=pl.BlockSpec((bm, bn), lambda j, i: (i, j)),
        out_shape=jax.ShapeDtypeStruct((m, n), F32),
        scratch_shapes=[pltpu.VMEM((k, bn), BF16)],
        compiler_params=_params("arbitrary", "arbitrary"),
        name="mm_resid",
    )(a, w, h, mod)


def _diff_attn_kernel(lq1_ref, lk1_ref, lq2_ref, lk2_ref, sub_ref, q_ref, k_ref, v_ref,
                      o_ref, m1, l1, a1, m2, l2, a2, *, kc, lambda_init):
    hd = DIFF_HEAD_DIM
    n_chunks = k_ref.shape[0] // kc
    for m_ref, l_ref, a_ref in ((m1, l1, a1), (m2, l2, a2)):
        m_ref[...] = jnp.full(m_ref.shape, -jnp.inf, F32)
        l_ref[...] = jnp.zeros(l_ref.shape, F32)
        a_ref[...] = jnp.zeros(a_ref.shape, F32)
    dn = (((1,), (1,)), ((), ()))

    def body(c, carry):
        rows = pl.ds(pl.multiple_of(c * kc, kc), kc)
        vs = v_ref[rows, :]
        for half, (m_ref, l_ref, a_ref) in enumerate(((m1, l1, a1), (m2, l2, a2))):
            cols = slice(half * hd, (half + 1) * hd)
            s = lax.dot_general(q_ref[:, cols], k_ref[rows, cols], dn, preferred_element_type=F32)
            m_old = m_ref[...]
            m_new = jnp.maximum(m_old, jnp.max(s, axis=-1, keepdims=True))
            alpha = jnp.exp(m_old - m_new)
            p = jnp.exp(s - m_new)
            l_ref[...] = alpha * l_ref[...] + jnp.sum(p, axis=-1, keepdims=True)
            a_ref[...] = alpha * a_ref[...] + jnp.dot(p.astype(BF16), vs, preferred_element_type=F32)
            m_ref[...] = m_new
        return carry

    lax.fori_loop(0, n_chunks, body, 0)
    lam = (jnp.exp(jnp.sum(lq1_ref[...] * lk1_ref[...], keepdims=True))
           - jnp.exp(jnp.sum(lq2_ref[...] * lk2_ref[...], keepdims=True))) + lambda_init
    o = a1[...] / l1[...] - lam * (a2[...] / l2[...])
    o_ref[...] = (_rms(o) * sub_ref[...] * (1.0 - lambda_init)).astype(o_ref.dtype)


def diff_attn(qkv, lam_vecs, subln, n_heads, q_rows, kv_rows, bq, kc, lambda_init):
    d = n_heads * DIFF_V_DIM
    q0, nq = q_rows
    k0, nk = kv_rows
    assert q0 % bq == 0 and nq % bq == 0 and k0 % nk == 0 and nk % kc == 0
    qb0, kb0 = q0 // bq, k0 // nk
    vec = pl.BlockSpec((1, LANES), lambda h, i: (0, 0))
    in_specs = [vec, vec, vec, vec,
                pl.BlockSpec((1, DIFF_V_DIM), lambda h, i: (0, 0)),
                pl.BlockSpec((bq, DIFF_V_DIM), lambda h, i: (qb0 + i, h)),
                pl.BlockSpec((nk, DIFF_V_DIM), lambda h, i: (kb0, n_heads + h)),
                pl.BlockSpec((nk, DIFF_V_DIM), lambda h, i: (kb0, 2 * n_heads + h))]
    args = [v.reshape(1, LANES) for v in lam_vecs] + [subln.reshape(1, DIFF_V_DIM), qkv, qkv, qkv]
    kern = functools.partial(_diff_attn_kernel, kc=kc, lambda_init=lambda_init)
    return pl.pallas_call(
        kern,
        grid=(n_heads, nq // bq),
        in_specs=in_specs,
        out_specs=pl.BlockSpec((bq, DIFF_V_DIM), lambda h, i: (i, h)),
        out_shape=jax.ShapeDtypeStruct((nq, d), BF16),
        scratch_shapes=[pltpu.VMEM((bq, 1), F32), pltpu.VMEM((bq, 1), F32), pltpu.VMEM((bq, DIFF_V_DIM), F32),
                        pltpu.VMEM((bq, 1), F32), pltpu.VMEM((bq, 1), F32), pltpu.VMEM((bq, DIFF_V_DIM), F32)],
        compiler_params=_params("arbitrary", "arbitrary"),
        name="diff_attn",
    )(*args)


def _retention_kernel(dec_ref, q_ref, k_ref, v_ref, g_ref, kc_ref, vc_ref, *rest, backward, has_prev):
    if has_prev:
        prev_ref, o_ref, s_ref, intra_ref, qd_ref, kd_ref, cd_ref = rest
    else:
        prev_ref = None
        o_ref, s_ref, intra_ref, qd_ref, kd_ref, cd_ref = rest
    c = RET_CHUNK
    n_chunks = q_ref.shape[0] // c
    dn_t = (((0,), (0,)), ((), ()))
    dn_nt = (((1,), (1,)), ((), ()))

    @pl.when(pl.program_id(1) == 0)
    def _():
        lg = jnp.log1p(-jnp.exp(dec_ref[0]))[0:1, 0:1]
        i = lax.broadcasted_iota(I32, (c, c), 0)
        j = lax.broadcasted_iota(I32, (c, c), 1)
        pos = lax.broadcasted_iota(I32, (c, 1), 0).astype(F32)
        rel = (j - i) if backward else (i - j)
        relf = jnp.maximum(rel, 0).astype(F32)
        intra_ref[...] = jnp.where(rel >= 0, jnp.exp(lg * relf), 0.0)
        if backward:
            qd_ref[...] = jnp.exp(lg * (c - pos))
            kd_ref[...] = jnp.exp(lg * pos)
        else:
            qd_ref[...] = jnp.exp(lg * (pos + 1.0))
            kd_ref[...] = jnp.exp(lg * (c - 1.0 - pos))
        cd_ref[...] = jnp.broadcast_to(jnp.exp(lg * float(c)), cd_ref.shape)
        kd = (kc_ref[...].astype(F32) * kd_ref[...]).astype(BF16)
        s_ref[...] = lax.dot_general(kd, vc_ref[...], dn_t, preferred_element_type=F32)

    def body(t, carry):
        cc = (n_chunks - 1 - t) if backward else t
        rows = pl.ds(pl.multiple_of(cc * c, c), c)
        q, k, v = q_ref[rows, :], k_ref[rows, :], v_ref[rows, :]
        scores = lax.dot_general(q, k, dn_nt, preferred_element_type=F32) * intra_ref[...]
        s_old = s_ref[...]
        qd = (q.astype(F32) * qd_ref[...]).astype(BF16)
        o = (jnp.dot(scores.astype(BF16), v, preferred_element_type=F32)
             + jnp.dot(qd, s_old.astype(BF16), preferred_element_type=F32))
        kd = (k.astype(F32) * kd_ref[...]).astype(BF16)
        s_ref[...] = s_old * cd_ref[0:1, 0:1] + lax.dot_general(kd, v, dn_t, preferred_element_type=F32)
        out = _silu(g_ref[rows, :].astype(F32)) * _rms(o)
        if has_prev:
            out = out + prev_ref[rows, :].astype(F32)
        o_ref[rows, :] = out.astype(o_ref.dtype)
        return carry

    lax.fori_loop(0, n_chunks, body, 0)


def retention_dir(proj, decay, n_heads, t_lat, backward, prev=None):
    c = RET_CHUNK
    d = n_heads * RET_QK_DIM
    t_ctx = proj.shape[0] - t_lat
    assert t_ctx == c and t_lat % c == 0
    rows = _largest_divisor(t_lat, (2048, 1024, 512, 256))
    n_steps = t_lat // rows
    step = (lambda s: n_steps - 1 - s) if backward else (lambda s: s)
    qb, vb = d // RET_QK_DIM, 2 * d // RET_V_DIM
    gb = (6 if backward else 4) * d // RET_V_DIM
    ctx_blk = t_lat // c
    in_specs = [pl.BlockSpec((1, 8, LANES), lambda h, s: (h, 0, 0)),
                pl.BlockSpec((rows, RET_QK_DIM), lambda h, s: (step(s), h)),
                pl.BlockSpec((rows, RET_QK_DIM), lambda h, s: (step(s), qb + h)),
                pl.BlockSpec((rows, RET_V_DIM), lambda h, s: (step(s), vb + h)),
                pl.BlockSpec((rows, RET_V_DIM), lambda h, s: (step(s), gb + h)),
                pl.BlockSpec((c, RET_QK_DIM), lambda h, s: (ctx_blk, qb + h)),
                pl.BlockSpec((c, RET_V_DIM), lambda h, s: (ctx_blk, vb + h))]
    args = [jnp.broadcast_to(decay.astype(F32)[:, None, None], (n_heads, 8, LANES)), proj, proj, proj, proj, proj, proj]
    if prev is not None:
        in_specs.append(pl.BlockSpec((rows, RET_V_DIM), lambda h, s: (step(s), h)))
        args.append(prev)
    kern = functools.partial(_retention_kernel, backward=backward, has_prev=prev is not None)
    return pl.pallas_call(
        kern,
        grid=(n_heads, n_steps),
        in_specs=in_specs,
        out_specs=pl.BlockSpec((rows, RET_V_DIM), lambda h, s: (step(s), h)),
        out_shape=jax.ShapeDtypeStruct((t_lat, n_heads * RET_V_DIM), BF16),
        scratch_shapes=[pltpu.VMEM((RET_QK_DIM, RET_V_DIM), F32), pltpu.VMEM((c, c), F32),
                        pltpu.VMEM((c, 1), F32), pltpu.VMEM((c, 1), F32), pltpu.VMEM((8, LANES), F32)],
        compiler_params=_params("arbitrary", "arbitrary"),
        name="retention_bwd" if backward else "retention_fwd",
    )(*args)


def _norm_logits_kernel(h_ref, gain_ref, sh_ref, sc_ref, wr_ref, lg_ref, *, n_lat_tiles):
    is_ctx = pl.program_id(0) >= n_lat_tiles
    y = _rms(h_ref[...]) * gain_ref[...]
    n2 = (y * (1.0 + _mod_rows(sc_ref, is_ctx)) + _mod_rows(sh_ref, is_ctx)).astype(BF16)
    lg_ref[...] = jnp.dot(n2, wr_ref[...].astype(BF16), preferred_element_type=F32)


def norm_logits(h, gain, mod, k_shift, k_scale, w_router, t_lat):
    rows, d = h.shape
    e = w_router.shape[1]
    return pl.pallas_call(
        functools.partial(_norm_logits_kernel, n_lat_tiles=t_lat // ROW_TILE),
        grid=(rows // ROW_TILE,),
        in_specs=[pl.BlockSpec((ROW_TILE, d), lambda i: (i, 0)),
                  pl.BlockSpec((1, d), lambda i: (0, 0)),
                  pl.BlockSpec((8, d), lambda i: (0, k_shift)),
                  pl.BlockSpec((8, d), lambda i: (0, k_scale)),
                  pl.BlockSpec((d, e), lambda i: (0, 0))],
        out_specs=pl.BlockSpec((ROW_TILE, e), lambda i: (i, 0)),
        out_shape=jax.ShapeDtypeStruct((rows, e), F32),
        compiler_params=_params("arbitrary"),
        name="norm_logits",
    )(h, gain.reshape(1, d), mod, mod, w_router)


def _route_kernel(lg_ref, idx_ref, gate_ref, bits_ref, key_ref, ahi_ref, amid_ref, alo_ref, tmat_ref, *, cap):
    t, e = lg_ref.shape
    pb = min(t, 256)
    tb = min(t, 1024)
    lg = lg_ref[...]
    ex = jnp.exp(lg - jnp.max(lg, axis=-1, keepdims=True))
    aff = ex / jnp.sum(ex, axis=-1, keepdims=True)
    bits_ref[...] = lax.bitcast_convert_type(aff, I32)
    ahi = aff.astype(BF16)
    r1 = aff - ahi.astype(F32)
    amid = r1.astype(BF16)
    ahi_ref[...] = ahi
    amid_ref[...] = amid
    alo_ref[...] = (r1 - amid.astype(F32)).astype(BF16)
    tok = lax.broadcasted_iota(I32, (t, e), 0)
    lane = lax.broadcasted_iota(I32, (t, e), 1)
    tmat_ref[...] = jnp.where(lane == 0, tok >> 7, jnp.where(lane == 1, tok & 127, 0)).astype(F32).astype(BF16)

    def thr_body(it, thr):
        cand = thr | (jnp.int32(1) << (30 - it))
        cnt = jnp.sum(jnp.where(bits_ref[...] >= cand, 1.0, 0.0), axis=0, keepdims=True)
        return jnp.where(cnt >= cap, cand, thr)

    thr = lax.fori_loop(0, 31, thr_body, jnp.zeros((1, e), I32))
    n_gt = jnp.sum(jnp.where(bits_ref[...] > thr, 1.0, 0.0), axis=0, keepdims=True)
    need = cap - n_gt

    ri = lax.broadcasted_iota(I32, (pb, pb), 0)
    ci = lax.broadcasted_iota(I32, (pb, pb), 1)
    ltri = jnp.where(ci < ri, 1.0, 0.0).astype(BF16)

    def pre_body(b, carry):
        ceq, csel = carry
        rows = pl.ds(pl.multiple_of(b * pb, pb), pb)
        bt = bits_ref[rows, :]
        gtb = bt > thr
        eqf = jnp.where(bt == thr, 1.0, 0.0)
        eq_rank = jnp.dot(ltri, eqf.astype(BF16), preferred_element_type=F32) + ceq
        self_ = jnp.where(gtb, 1.0, jnp.where(eq_rank < need, eqf, 0.0))
        pos = jnp.dot(ltri, self_.astype(BF16), preferred_element_type=F32) + csel
        key_ref[rows, :] = jnp.where(self_ > 0.0, pos, -1.0).astype(I32)
        return (ceq + jnp.sum(eqf, axis=0, keepdims=True), csel + jnp.sum(self_, axis=0, keepdims=True))

    zero = jnp.zeros((1, e), F32)
    lax.fori_loop(0, t // pb, pre_body, (zero, zero))

    slot = lax.broadcasted_iota(I32, (1, cap), 1)
    dn_t = (((0,), (0,)), ((), ()))
    for ex_i in range(e):
        def slot_body(b, carry, ex_i=ex_i):
            acc_g, acc_t = carry
            rows = pl.ds(pl.multiple_of(b * tb, tb), tb)
            onehot = jnp.where(key_ref[rows, ex_i:ex_i + 1] == slot, 1.0, 0.0).astype(BF16)
            part = (lax.dot_general(ahi_ref[rows, :], onehot, dn_t, preferred_element_type=F32)
                    + lax.dot_general(amid_ref[rows, :], onehot, dn_t, preferred_element_type=F32)
                    + lax.dot_general(alo_ref[rows, :], onehot, dn_t, preferred_element_type=F32))
            tpart = lax.dot_general(tmat_ref[rows, :], onehot, dn_t, preferred_element_type=F32)
            return acc_g + part, acc_t + tpart

        z = jnp.zeros((e, cap), F32)
        acc_g, acc_t = lax.fori_loop(0, t // tb, slot_body, (z, z))
        gate_ref[ex_i:ex_i + 1, :] = acc_g[ex_i:ex_i + 1, :]
        idx_ref[ex_i:ex_i + 1, :] = (acc_t[0:1, :] * 128.0 + acc_t[1:2, :]).astype(I32)


def route(logits, cap):
    t, e = logits.shape
    return pl.pallas_call(
        functools.partial(_route_kernel, cap=cap),
        out_shape=(jax.ShapeDtypeStruct((e, cap), I32), jax.ShapeDtypeStruct((e, cap), F32)),
        scratch_shapes=[pltpu.VMEM((t, e), I32), pltpu.VMEM((t, e), I32), pltpu.VMEM((t, e), BF16),
                        pltpu.VMEM((t, e), BF16), pltpu.VMEM((t, e), BF16), pltpu.VMEM((t, e), BF16)],
        compiler_params=pltpu.CompilerParams(vmem_limit_bytes=VMEM_LIMIT),
        name="route",
    )(logits)


def _gather_kernel(idx_sm, h_hbm, idxc_ref, gain_ref, sh_ref, sc_ref, o_ref, buf, sem, *, cb, t_lat):
    ex_i, b = pl.program_id(0), pl.program_id(1)
    base = b * cb

    def row_copy(r, tok):
        return pltpu.make_async_copy(h_hbm.at[pl.ds(tok, 1), :], buf.at[pl.ds(r, 1), :], sem)

    def start(r, carry):
        row_copy(r, idx_sm[ex_i, base + r]).start()
        return carry

    def wait(r, carry):
        row_copy(r, 0).wait()
        return carry

    lax.fori_loop(0, cb, start, 0)
    lax.fori_loop(0, cb, wait, 0)
    is_ctx = idxc_ref[0] >= t_lat
    y = _rms(buf[...]) * gain_ref[...]
    sc = jnp.where(is_ctx, sc_ref[1:2, :], sc_ref[0:1, :])
    sh = jnp.where(is_ctx, sh_ref[1:2, :], sh_ref[0:1, :])
    o_ref[0] = (y * (1.0 + sc) + sh).astype(o_ref.dtype)


def moe_gather(h, idx, gain, mod, k_shift, k_scale, t_lat, cb):
    e, ct = idx.shape
    d = h.shape[1]
    grid_spec = pltpu.PrefetchScalarGridSpec(
        num_scalar_prefetch=1,
        grid=(e, ct // cb),
        in_specs=[pl.BlockSpec(memory_space=pl.ANY),
                  pl.BlockSpec((1, cb, 1), lambda i, b, idx_sm: (i, b, 0)),
                  pl.BlockSpec((1, d), lambda i, b, idx_sm: (0, 0)),
                  pl.BlockSpec((8, d), lambda i, b, idx_sm: (0, k_shift)),
                  pl.BlockSpec((8, d), lambda i, b, idx_sm: (0, k_scale))],
        out_specs=pl.BlockSpec((1, cb, d), lambda i, b, idx_sm: (i, b, 0)),
        scratch_shapes=[pltpu.VMEM((cb, d), F32), pltpu.SemaphoreType.DMA(())],
    )
    return pl.pallas_call(
        functools.partial(_gather_kernel, cb=cb, t_lat=t_lat),
        grid_spec=grid_spec,
        out_shape=jax.ShapeDtypeStruct((e, ct, d), BF16),
        compiler_params=_params("arbitrary", "arbitrary"),
        name="moe_gather",
    )(idx, h, idx[:, :, None], gain.reshape(1, d), mod, mod)


def _ffn_up_kernel(x_ref, wg_ref, wu_ref, o_ref):
    x = x_ref[0]
    g = jnp.dot(x, wg_ref[0].astype(BF16), preferred_element_type=F32)
    u = jnp.dot(x, wu_ref[0].astype(BF16), preferred_element_type=F32)
    o_ref[0] = (_silu(g) * u).astype(o_ref.dtype)


def ffn_up(xg, w_gate, w_up):
    e, ct, d = xg.shape
    f = w_gate.shape[2]
    fb = 256
    return pl.pallas_call(
        _ffn_up_kernel,
        grid=(e, f // fb),
        in_specs=[pl.BlockSpec((1, ct, d), lambda i, j: (i, 0, 0)),
                  pl.BlockSpec((1, d, fb), lambda i, j: (i, 0, j)),
                  pl.BlockSpec((1, d, fb), lambda i, j: (i, 0, j))],
        out_specs=pl.BlockSpec((1, ct, fb), lambda i, j: (i, 0, j)),
        out_shape=jax.ShapeDtypeStruct((e, ct, f), BF16),
        compiler_params=_params("arbitrary", "arbitrary"),
        name="ffn_up",
    )(xg, w_gate, w_up)


def _ffn_down_kernel(h_ref, wd_ref, gate_ref, idxc_ref, g2_ref, o_ref, *, t_lat):
    acc = jnp.dot(h_ref[0], wd_ref[0].astype(BF16), preferred_element_type=F32)
    g2 = jnp.where(idxc_ref[0] >= t_lat, g2_ref[1:2, :], g2_ref[0:1, :])
    o_ref[0] = (acc * gate_ref[0] * g2).astype(o_ref.dtype)


def ffn_down(hid, w_down, gate, idx, mod, k_gate, t_lat):
    e, ct, f = hid.shape
    d = w_down.shape[2]
    nb = _largest_divisor(d, (1024, 512, 256, 128))
    nblk = d // nb
    return pl.pallas_call(
        functools.partial(_ffn_down_kernel, t_lat=t_lat),
        grid=(e, nblk),
        in_specs=[pl.BlockSpec((1, ct, f), lambda i, j: (i, 0, 0)),
                  pl.BlockSpec((1, f, nb), lambda i, j: (i, 0, j)),
                  pl.BlockSpec((1, ct, 1), lambda i, j: (i, 0, 0)),
                  pl.BlockSpec((1, ct, 1), lambda i, j: (i, 0, 0)),
                  pl.BlockSpec((8, nb), lambda i, j: (0, k_gate * nblk + j))],
        out_specs=pl.BlockSpec((1, ct, nb), lambda i, j: (i, 0, j)),
        out_shape=jax.ShapeDtypeStruct((e, ct, d), BF16),
        compiler_params=_params("arbitrary", "arbitrary"),
        name="ffn_down",
    )(hid, w_down, gate[:, :, None], idx[:, :, None], mod)


def _scatter_kernel(idx_sm, y_ref, acc_in, acc_out, buf, sem_in, sem_out, *, cb):
    del acc_in
    ex_i, b = pl.program_id(0), pl.program_id(1)
    base = b * cb

    def fetch(r, tok):
        return pltpu.make_async_copy(acc_out.at[pl.ds(tok, 1), :], buf.at[pl.ds(r, 1), :], sem_in)

    def put(r, tok):
        return pltpu.make_async_copy(buf.at[pl.ds(r, 1), :], acc_out.at[pl.ds(tok, 1), :], sem_out)

    def loop(fn):
        lax.fori_loop(0, cb, lambda r, carry: (fn(r), carry)[1], 0)

    loop(lambda r: fetch(r, idx_sm[ex_i, base + r]).start())
    loop(lambda r: fetch(r, 0).wait())
    buf[...] = buf[...] + y_ref[0].astype(F32)
    loop(lambda r: put(r, idx_sm[ex_i, base + r]).start())
    loop(lambda r: put(r, 0).wait())


def moe_scatter_add(acc, idx, y, cb):
    e, ct = idx.shape
    rows, d = acc.shape
    grid_spec = pltpu.PrefetchScalarGridSpec(
        num_scalar_prefetch=1,
        grid=(e, ct // cb),
        in_specs=[pl.BlockSpec((1, cb, d), lambda i, b, idx_sm: (i, b, 0)),
                  pl.BlockSpec(memory_space=pl.ANY)],
        out_specs=pl.BlockSpec(memory_space=pl.ANY),
        scratch_shapes=[pltpu.VMEM((cb, d), F32), pltpu.SemaphoreType.DMA(()), pltpu.SemaphoreType.DMA(())],
    )
    return pl.pallas_call(
        functools.partial(_scatter_kernel, cb=cb),
        grid_spec=grid_spec,
        out_shape=jax.ShapeDtypeStruct((rows, d), F32),
        input_output_aliases={2: 0},
        compiler_params=_params("arbitrary", "arbitrary"),
        name="moe_scatter_add",
    )(idx, y, acc)


def expert_choice_moe(h, gain, mod, w_router, w_gate, w_up, w_down, groups, t_lat):
    e = w_router.shape[1]
    logits = norm_logits(h, gain, mod, 3, 4, w_router, t_lat)
    idx_parts, gate_parts = [], []
    for start, count in groups:
        idx_g, gate_g = route(logits[start:start + count], EC_CAPACITY_FACTOR * count // e)
        idx_parts.append(idx_g + start)
        gate_parts.append(gate_g)
    idx = jnp.concatenate(idx_parts, axis=1)
    gate = jnp.concatenate(gate_parts, axis=1)
    ct = idx.shape[1]
    cb = _largest_divisor(ct, (352, 256, 128, 96, 64, 32, 16))
    xg = moe_gather(h, idx, gain, mod, 3, 4, t_lat, cb)
    hid = ffn_up(xg, w_gate, w_up)
    y = ffn_down(hid, w_down, gate, idx, mod, 5, t_lat)
    return moe_scatter_add(h, idx, y, cb)


def _axial_tables(t_lat, t_ctx, rot_dim):
    axis_dim = rot_dim // 2
    tok = jnp.arange(t_lat)
    row = (tok // GRID_W).astype(F32)
    col = (tok % GRID_W).astype(F32)
    inv_freq = ROPE_BASE ** (-jnp.arange(0, axis_dim, 2, dtype=F32) / axis_dim)
    ar, ac = row[:, None] * inv_freq, col[:, None] * inv_freq
    cos = jnp.concatenate([jnp.cos(ar), jnp.cos(ar), jnp.cos(ac), jnp.cos(ac)], axis=-1)
    sin = jnp.concatenate([-jnp.sin(ar), jnp.sin(ar), -jnp.sin(ac), jnp.sin(ac)], axis=-1)
    cos = jnp.concatenate([cos, jnp.ones((t_ctx, rot_dim), F32)], axis=0)
    sin = jnp.concatenate([sin, jnp.zeros((t_ctx, rot_dim), F32)], axis=0)
    return cos, sin


def kernel(x, c, ctx, c_ctx, ada_w, ada_b, norm_mix, norm_ffn, diff_w_in, diff_w_out, diff_q_norm, diff_k_norm, diff_lambda_q1, diff_lambda_k1, diff_lambda_q2, diff_lambda_k2, diff_subln, ret_w_in, ret_w_out, ret_decay_fwd, ret_decay_bwd, router_w, expert_w_gate, expert_w_up, expert_w_down):
    batch, t_lat, d = x.shape
    t_ctx = ctx.shape[1]
    depth = ada_w.shape[0]
    assert batch == 1 and depth == 2 and t_lat % ROW_TILE == 0 and t_ctx == ROW_TILE
    t_all = t_lat + t_ctx
    diff_heads = d // DIFF_V_DIM
    ret_heads = d // RET_QK_DIM

    h = jnp.concatenate([x[0], ctx[0]], axis=0)
    cvec = jnp.zeros((8, d), F32).at[0].set(c[0]).at[1].set(c_ctx)
    cos_d, sin_d = _axial_tables(t_lat, t_ctx, DIFF_HEAD_DIM)
    cos_r, sin_r = _axial_tables(t_lat, t_ctx, RET_QK_DIM)

    mod = adaln(cvec, ada_w[0], ada_b[0])
    n1 = norm_mod(h, norm_mix[0], mod, 0, 1, t_lat)
    qkv = mm_diff_in(n1, diff_w_in[0], diff_q_norm[0], diff_k_norm[0], cos_d, sin_d)
    lam_vecs = (diff_lambda_q1[0], diff_lambda_k1[0], diff_lambda_q2[0], diff_lambda_k2[0])
    lambda_init = 0.8 - 0.6 * math.exp(-0.3 * 0)
    bq = _largest_divisor(t_lat, (512, 256))
    kc = _largest_divisor(t_all, (768, 256))
    att_lat = diff_attn(qkv, lam_vecs, diff_subln[0], diff_heads, (0, t_lat), (0, t_all), bq, kc, lambda_init)
    att_ctx = diff_attn(qkv, lam_vecs, diff_subln[0], diff_heads, (t_lat, t_ctx), (t_lat, t_ctx), t_ctx, t_ctx,
                        lambda_init)
    h = mm_resid(jnp.concatenate([att_lat, att_ctx], axis=0), diff_w_out[0], h, mod, 2, t_lat)
    h = expert_choice_moe(h, norm_ffn[0], mod, router_w[0], expert_w_gate[0], expert_w_up[0], expert_w_down[0],
                          [(0, t_lat), (t_lat, t_ctx)], t_lat)

    mod = adaln(cvec, ada_w[1], ada_b[1])
    n1 = norm_mod(h, norm_mix[1], mod, 0, 1, t_lat)
    proj = mm_ret_in(n1, ret_w_in[0], cos_r, sin_r)
    gated = retention_dir(proj, ret_decay_fwd[0], ret_heads, t_lat, backward=False)
    gated = retention_dir(proj, ret_decay_bwd[0], ret_heads, t_lat, backward=True, prev=gated)
    h = mm_resid(gated, ret_w_out[0], h, mod, 2, t_lat, single_buffer_w=True)
    h = expert_choice_moe(h, norm_ffn[1], mod, router_w[1], expert_w_gate[1], expert_w_up[1], expert_w_down[1],
                          [(0, t_lat)], t_lat)
    return h[None]
```

```python
import functools
import math

import jax
import jax.numpy as jnp
from jax import lax
from jax.experimental import pallas as pl
from jax.experimental.pallas import tpu as pltpu

F32 = jnp.float32
BF16 = jnp.bfloat16
I32 = jnp.int32

NORM_EPS = 1e-6
GRID_W = 64
ROPE_BASE = 10000.0
DIFF_HEAD_DIM = 128
DIFF_V_DIM = 2 * DIFF_HEAD_DIM
RET_QK_DIM = 256
RET_V_DIM = 2 * RET_QK_DIM
RET_CHUNK = 256
EC_CAPACITY_FACTOR = 2
ROW_TILE = 256
ROW_GROUP = 16
DMA_UNROLL = 8
LANES = 128
V7X_VMEM_BYTES = 64 * 1024 * 1024
VMEM_LIMIT = V7X_VMEM_BYTES - 8 * 1024 * 1024


def _params(*sem):
    return pltpu.CompilerParams(dimension_semantics=sem, vmem_limit_bytes=VMEM_LIMIT)


def _silu(x):
    return x * (1.0 / (1.0 + jnp.exp(-x)))


def _rms(x):
    return x * lax.rsqrt(jnp.mean(x * x, axis=-1, keepdims=True) + NORM_EPS)


def _largest_divisor(n, candidates):
    for cand in candidates:
        if n % cand == 0:
            return cand
    raise ValueError(f"no tile in {candidates} divides {n}")


def _mm_tiles(m, k):
    bn = 512 if k <= 4096 else 256
    bm = _largest_divisor(m, (1024, 768, 512, 256) if k <= 4096 else (512, 256))
    return bm, bn


def _adaln_kernel(c_ref, w_ref, b_ref, o_ref):
    s = _silu(c_ref[...])
    o_ref[...] = jnp.dot(s.astype(BF16), w_ref[...].astype(BF16), preferred_element_type=F32) + b_ref[...]


def adaln(cvec, w, b, li):
    _, d, n = w.shape
    bn = 512
    return pl.pallas_call(
        _adaln_kernel,
        grid=(n // bn,),
        in_specs=[pl.BlockSpec((8, d), lambda j: (0, 0)),
                  pl.BlockSpec((None, d, bn), lambda j: (li, 0, j)),
                  pl.BlockSpec((None, 1, bn), lambda j: (li, 0, j))],
        out_specs=pl.BlockSpec((8, bn), lambda j: (0, j)),
        out_shape=jax.ShapeDtypeStruct((8, n), F32),
        compiler_params=_params("arbitrary"),
        name="adaln",
    )(cvec, w, b[:, None, :])


def _mod_rows(mod_ref, is_ctx):
    return jnp.where(is_ctx, mod_ref[1:2, :], mod_ref[0:1, :])


def _norm_mod_kernel(h_ref, gain_ref, sh_ref, sc_ref, o_ref, *, n_lat_tiles):
    is_ctx = pl.program_id(0) >= n_lat_tiles
    y = _rms(h_ref[...]) * gain_ref[...]
    o_ref[...] = (y * (1.0 + _mod_rows(sc_ref, is_ctx)) + _mod_rows(sh_ref, is_ctx)).astype(o_ref.dtype)


def norm_mod(h, gain, mod, k_shift, k_scale, t_lat):
    rows, d = h.shape
    return pl.pallas_call(
        functools.partial(_norm_mod_kernel, n_lat_tiles=t_lat // ROW_TILE),
        grid=(rows // ROW_TILE,),
        in_specs=[pl.BlockSpec((ROW_TILE, d), lambda i: (i, 0)),
                  pl.BlockSpec((1, d), lambda i: (0, 0)),
                  pl.BlockSpec((8, d), lambda i: (0, k_shift)),
                  pl.BlockSpec((8, d), lambda i: (0, k_scale))],
        out_specs=pl.BlockSpec((ROW_TILE, d), lambda i: (i, 0)),
        out_shape=jax.ShapeDtypeStruct((rows, d), BF16),
        compiler_params=_params("arbitrary"),
        name="norm_mod",
    )(h, gain.reshape(1, d), mod, mod)


def _mm_acc(a_ref, w_ref, wbf_ref):
    @pl.when(pl.program_id(1) == 0)
    def _():
        wbf_ref[...] = w_ref[...].astype(BF16)

    return jnp.dot(a_ref[...], wbf_ref[...], preferred_element_type=F32)


def _rot_half_64(x):
    lane = lax.broadcasted_iota(I32, (1, LANES), 1)
    first = (lane % 64) < 32
    return jnp.where(first, pltpu.roll(x, 96, 1), pltpu.roll(x, 32, 1))


def _mm_diff_in_kernel(a_ref, w_ref, qg_ref, kg_ref, cos_ref, sin_ref, o_ref, wbf_ref, *, nq, nqk, qscale):
    j = pl.program_id(0)
    acc = _mm_acc(a_ref, w_ref, wbf_ref)

    @pl.when(j < nqk)
    def _():
        is_q = j < nq
        gain = jnp.where(is_q, qg_ref[...], kg_ref[...])
        scale = jnp.where(is_q, qscale, 1.0)
        cos, sin = cos_ref[...], sin_ref[...]
        for cidx in range(acc.shape[1] // LANES):
            sl = slice(cidx * LANES, (cidx + 1) * LANES)
            y = _rms(acc[:, sl]) * gain
            y = y * cos + _rot_half_64(y) * sin
            o_ref[:, sl] = (y * scale).astype(o_ref.dtype)

    @pl.when(j >= nqk)
    def _():
        o_ref[...] = acc.astype(o_ref.dtype)


def mm_diff_in(a, w, li, q_gain, k_gain, cos, sin):
    m, k = a.shape
    n = w.shape[2]
    d = n // 3
    bm, bn = _mm_tiles(m, k)
    kern = functools.partial(_mm_diff_in_kernel, nq=d // bn, nqk=2 * d // bn, qscale=DIFF_HEAD_DIM ** -0.5)
    return pl.pallas_call(
        kern,
        grid=(n // bn, m // bm),
        in_specs=[pl.BlockSpec((bm, k), lambda j, i: (i, 0)),
                  pl.BlockSpec((None, k, bn), lambda j, i: (li, 0, j)),
                  pl.BlockSpec((1, LANES), lambda j, i: (0, 0)),
                  pl.BlockSpec((1, LANES), lambda j, i: (0, 0)),
                  pl.BlockSpec((bm, LANES), lambda j, i: (i, 0)),
                  pl.BlockSpec((bm, LANES), lambda j, i: (i, 0))],
        out_specs=pl.BlockSpec((bm, bn), lambda j, i: (i, j)),
        out_shape=jax.ShapeDtypeStruct((m, n), BF16),
        scratch_shapes=[pltpu.VMEM((k, bn), BF16)],
        compiler_params=_params("arbitrary", "arbitrary"),
        name="mm_diff_in",
    )(a, w, q_gain.reshape(1, LANES), k_gain.reshape(1, LANES), cos, sin)


def _mm_ret_in_kernel(a_ref, w_ref, cos_ref, sin_ref, o_ref, wbf_ref, *, nq, nqk, kscale):
    j = pl.program_id(0)
    acc = _mm_acc(a_ref, w_ref, wbf_ref)

    @pl.when(j < nqk)
    def _():
        scale = jnp.where(j < nq, 1.0, kscale)
        for cidx in range(acc.shape[1] // LANES):
            sl = slice(cidx * LANES, (cidx + 1) * LANES)
            tl = slice((cidx % 2) * LANES, (cidx % 2 + 1) * LANES)
            x = acc[:, sl]
            y = x * cos_ref[:, tl] + pltpu.roll(x, 64, 1) * sin_ref[:, tl]
            o_ref[:, sl] = (y * scale).astype(o_ref.dtype)

    @pl.when(j >= nqk)
    def _():
        o_ref[...] = acc.astype(o_ref.dtype)


def mm_ret_in(a, w, li, cos, sin):
    m, k = a.shape
    n = w.shape[2]
    d = n // 8
    bm, bn = _mm_tiles(m, k)
    kern = functools.partial(_mm_ret_in_kernel, nq=d // bn, nqk=2 * d // bn, kscale=RET_QK_DIM ** -0.5)
    return pl.pallas_call(
        kern,
        grid=(n // bn, m // bm),
        in_specs=[pl.BlockSpec((bm, k), lambda j, i: (i, 0)),
                  pl.BlockSpec((None, k, bn), lambda j, i: (li, 0, j)),
                  pl.BlockSpec((bm, RET_QK_DIM), lambda j, i: (i, 0)),
                  pl.BlockSpec((bm, RET_QK_DIM), lambda j, i: (i, 0))],
        out_specs=pl.BlockSpec((bm, bn), lambda j, i: (i, j)),
        out_shape=jax.ShapeDtypeStruct((m, n), BF16),
        scratch_shapes=[pltpu.VMEM((k, bn), BF16)],
        compiler_params=_params("arbitrary", "arbitrary"),
        name="mm_ret_in",
    )(a, w, cos, sin)


def _mm_resid_kernel(a_ref, w_ref, h_ref, g_ref, o_ref, wbf_ref, *, t_lat):
    acc = _mm_acc(a_ref, w_ref, wbf_ref)
    bm = acc.shape[0]
    row = pl.program_id(1) * bm + lax.broadcasted_iota(I32, (bm, 1), 0)
    gate = jnp.where(row >= t_lat, g_ref[1:2, :], g_ref[0:1, :])
    o_ref[...] = h_ref[...] + gate * acc


def mm_resid(a, w, li, h, mod, k_gate, t_lat, single_buffer_w=False):
    m, k = a.shape
    n = w.shape[2]
    bm, bn = _mm_tiles(m, k)
    if single_buffer_w:
        bn = 512
        w_spec = pl.BlockSpec((None, k, bn), lambda j, i: (li, 0, j), pipeline_mode=pl.Buffered(1))
    else:
        w_spec = pl.BlockSpec((None, k, bn), lambda j, i: (li, 0, j))
    nb = n // bn
    return pl.pallas_call(
        functools.partial(_mm_resid_kernel, t_lat=t_lat),
        grid=(nb, m // bm),
        in_specs=[pl.BlockSpec((bm, k), lambda j, i: (i, 0)),
                  w_spec,
                  pl.BlockSpec((bm, bn), lambda j, i: (i, j)),
                  pl.BlockSpec((8, bn), lambda j, i: (0, k_gate * nb + j))],
        out_specs=pl.BlockSpec((bm, bn), lambda j, i: (i, j)),
        out_shape=jax.ShapeDtypeStruct((m, n), F32),
        scratch_shapes=[pltpu.VMEM((k, bn), BF16)],
        compiler_params=_params("arbitrary", "arbitrary"),
        name="mm_resid",
    )(a, w, h, mod)


def _diff_attn_kernel(lq1_ref, lk1_ref, lq2_ref, lk2_ref, sub_ref, q_ref, k_ref, v_ref,
                      o_ref, s_a, s_b, m1, l1, a1, m2, l2, a2, *, kc, lambda_init):
    hd = DIFF_HEAD_DIM
    n_chunks = k_ref.shape[0] // kc
    maps = ((m1, l1, a1), (m2, l2, a2))
    for m_ref, l_ref, a_ref in maps:
        m_ref[...] = jnp.full(m_ref.shape, -jnp.inf, F32)
        l_ref[...] = jnp.zeros(l_ref.shape, F32)
        a_ref[...] = jnp.zeros(a_ref.shape, F32)
    dn = (((1,), (1,)), ((), ()))

    def chunk_rows(c):
        return pl.ds(pl.multiple_of(c * kc, kc), kc)

    def scores(c, s_ref):
        rows = chunk_rows(c)
        for half in range(2):
            cols = slice(half * hd, (half + 1) * hd)
            s_ref[half] = lax.dot_general(q_ref[:, cols], k_ref[rows, cols], dn, preferred_element_type=F32)

    def update(c, s_ref):
        vs = v_ref[chunk_rows(c), :]
        for half, (m_ref, l_ref, a_ref) in enumerate(maps):
            s = s_ref[half]
            m_old = m_ref[...]
            m_new = jnp.maximum(m_old, jnp.max(s, axis=-1, keepdims=True))
            alpha = jnp.exp(m_old - m_new)
            p = jnp.exp(s - m_new)
            l_ref[...] = alpha * l_ref[...] + jnp.sum(p, axis=-1, keepdims=True)
            a_ref[...] = alpha * a_ref[...] + jnp.dot(p.astype(BF16), vs, preferred_element_type=F32)
            m_ref[...] = m_new

    scores(0, s_a)
    n_pairs = (n_chunks - 1) // 2

    def pair(i, carry):
        c = 2 * i
        scores(c + 1, s_b)
        update(c, s_a)
        scores(c + 2, s_a)
        update(c + 1, s_b)
        return carry

    lax.fori_loop(0, n_pairs, pair, 0)
    if n_chunks % 2 == 0:
        scores(n_chunks - 1, s_b)
        update(n_chunks - 2, s_a)
        update(n_chunks - 1, s_b)
    else:
        update(n_chunks - 1, s_a)

    lam = (jnp.exp(jnp.sum(lq1_ref[...] * lk1_ref[...], keepdims=True))
           - jnp.exp(jnp.sum(lq2_ref[...] * lk2_ref[...], keepdims=True))) + lambda_init
    o = a1[...] / l1[...] - lam * (a2[...] / l2[...])
    o_ref[...] = (_rms(o) * sub_ref[...] * (1.0 - lambda_init)).astype(o_ref.dtype)


def diff_attn(qkv, lam_vecs, subln, n_heads, q_rows, kv_rows, bq, kc, lambda_init):
    d = n_heads * DIFF_V_DIM
    q0, nq = q_rows
    k0, nk = kv_rows
    assert q0 % bq == 0 and nq % bq == 0 and k0 % nk == 0 and nk % kc == 0
    qb0, kb0 = q0 // bq, k0 // nk
    vec = pl.BlockSpec((1, LANES), lambda h, i: (0, 0))
    in_specs = [vec, vec, vec, vec,
                pl.BlockSpec((1, DIFF_V_DIM), lambda h, i: (0, 0)),
                pl.BlockSpec((bq, DIFF_V_DIM), lambda h, i: (qb0 + i, h)),
                pl.BlockSpec((nk, DIFF_V_DIM), lambda h, i: (kb0, n_heads + h)),
                pl.BlockSpec((nk, DIFF_V_DIM), lambda h, i: (kb0, 2 * n_heads + h))]
    args = [v.reshape(1, LANES) for v in lam_vecs] + [subln.reshape(1, DIFF_V_DIM), qkv, qkv, qkv]
    kern = functools.partial(_diff_attn_kernel, kc=kc, lambda_init=lambda_init)
    return pl.pallas_call(
        kern,
        grid=(n_heads, nq // bq),
        in_specs=in_specs,
        out_specs=pl.BlockSpec((bq, DIFF_V_DIM), lambda h, i: (i, h)),
        out_shape=jax.ShapeDtypeStruct((nq, d), BF16),
        scratch_shapes=[pltpu.VMEM((2, bq, kc), F32), pltpu.VMEM((2, bq, kc), F32),
                        pltpu.VMEM((bq, 1), F32), pltpu.VMEM((bq, 1), F32), pltpu.VMEM((bq, DIFF_V_DIM), F32),
                        pltpu.VMEM((bq, 1), F32), pltpu.VMEM((bq, 1), F32), pltpu.VMEM((bq, DIFF_V_DIM), F32)],
        compiler_params=_params("arbitrary", "arbitrary"),
        name="diff_attn",
    )(*args)


def _retention_kernel(dec_ref, q_ref, k_ref, v_ref, g_ref, kc_ref, vc_ref, *rest, backward, has_prev):
    if has_prev:
        prev_ref, o_ref, s_ref, intra_ref, qd_ref, kd_ref, cd_ref = rest
    else:
        prev_ref = None
        o_ref, s_ref, intra_ref, qd_ref, kd_ref, cd_ref = rest
    c = RET_CHUNK
    n_chunks = q_ref.shape[0] // c
    dn_t = (((0,), (0,)), ((), ()))
    dn_nt = (((1,), (1,)), ((), ()))

    @pl.when(pl.program_id(1) == 0)
    def _():
        lg = jnp.log1p(-jnp.exp(dec_ref[0]))[0:1, 0:1]
        i = lax.broadcasted_iota(I32, (c, c), 0)
        j = lax.broadcasted_iota(I32, (c, c), 1)
        pos = lax.broadcasted_iota(I32, (c, 1), 0).astype(F32)
        rel = (j - i) if backward else (i - j)
        relf = jnp.maximum(rel, 0).astype(F32)
        intra_ref[...] = jnp.where(rel >= 0, jnp.exp(lg * relf), 0.0)
        if backward:
            qd_ref[...] = jnp.exp(lg * (c - pos))
            kd_ref[...] = jnp.exp(lg * pos)
        else:
            qd_ref[...] = jnp.exp(lg * (pos + 1.0))
            kd_ref[...] = jnp.exp(lg * (c - 1.0 - pos))
        cd_ref[...] = jnp.broadcast_to(jnp.exp(lg * float(c)), cd_ref.shape)
        kd = (kc_ref[...].astype(F32) * kd_ref[...]).astype(BF16)
        s_ref[...] = lax.dot_general(kd, vc_ref[...], dn_t, preferred_element_type=F32)

    def body(t, carry):
        cc = (n_chunks - 1 - t) if backward else t
        rows = pl.ds(pl.multiple_of(cc * c, c), c)
        q, k, v = q_ref[rows, :], k_ref[rows, :], v_ref[rows, :]
        scores = lax.dot_general(q, k, dn_nt, preferred_element_type=F32) * intra_ref[...]
        s_old = s_ref[...]
        qd = (q.astype(F32) * qd_ref[...]).astype(BF16)
        o = (jnp.dot(scores.astype(BF16), v, preferred_element_type=F32)
             + jnp.dot(qd, s_old.astype(BF16), preferred_element_type=F32))
        kd = (k.astype(F32) * kd_ref[...]).astype(BF16)
        s_ref[...] = s_old * cd_ref[0:1, 0:1] + lax.dot_general(kd, v, dn_t, preferred_element_type=F32)
        out = _silu(g_ref[rows, :].astype(F32)) * _rms(o)
        if has_prev:
            out = out + prev_ref[rows, :].astype(F32)
        o_ref[rows, :] = out.astype(o_ref.dtype)
        return carry

    lax.fori_loop(0, n_chunks, body, 0)


def retention_dir(proj, decay, n_heads, t_lat, backward, prev=None):
    c = RET_CHUNK
    d = n_heads * RET_QK_DIM
    t_ctx = proj.shape[0] - t_lat
    assert t_ctx == c and t_lat % c == 0
    rows = _largest_divisor(t_lat, (2048, 1024, 512, 256))
    n_steps = t_lat // rows
    step = (lambda s: n_steps - 1 - s) if backward else (lambda s: s)
    qb, vb = d // RET_QK_DIM, 2 * d // RET_V_DIM
    gb = (6 if backward else 4) * d // RET_V_DIM
    ctx_blk = t_lat // c
    in_specs = [pl.BlockSpec((1, 8, LANES), lambda h, s: (h, 0, 0)),
                pl.BlockSpec((rows, RET_QK_DIM), lambda h, s: (step(s), h)),
                pl.BlockSpec((rows, RET_QK_DIM), lambda h, s: (step(s), qb + h)),
                pl.BlockSpec((rows, RET_V_DIM), lambda h, s: (step(s), vb + h)),
                pl.BlockSpec((rows, RET_V_DIM), lambda h, s: (step(s), gb + h)),
                pl.BlockSpec((c, RET_QK_DIM), lambda h, s: (ctx_blk, qb + h)),
                pl.BlockSpec((c, RET_V_DIM), lambda h, s: (ctx_blk, vb + h))]
    args = [jnp.broadcast_to(decay.astype(F32)[:, None, None], (n_heads, 8, LANES)), proj, proj, proj, proj, proj, proj]
    if prev is not None:
        in_specs.append(pl.BlockSpec((rows, RET_V_DIM), lambda h, s: (step(s), h)))
        args.append(prev)
    kern = functools.partial(_retention_kernel, backward=backward, has_prev=prev is not None)
    return pl.pallas_call(
        kern,
        grid=(n_heads, n_steps),
        in_specs=in_specs,
        out_specs=pl.BlockSpec((rows, RET_V_DIM), lambda h, s: (step(s), h)),
        out_shape=jax.ShapeDtypeStruct((t_lat, n_heads * RET_V_DIM), BF16),
        scratch_shapes=[pltpu.VMEM((RET_QK_DIM, RET_V_DIM), F32), pltpu.VMEM((c, c), F32),
                        pltpu.VMEM((c, 1), F32), pltpu.VMEM((c, 1), F32), pltpu.VMEM((8, LANES), F32)],
        compiler_params=_params("arbitrary", "arbitrary"),
        name="retention_bwd" if backward else "retention_fwd",
    )(*args)


def _norm_logits_kernel(h_ref, gain_ref, sh_ref, sc_ref, wr_ref, lg_ref, *, n_lat_tiles):
    is_ctx = pl.program_id(0) >= n_lat_tiles
    y = _rms(h_ref[...]) * gain_ref[...]
    n2 = (y * (1.0 + _mod_rows(sc_ref, is_ctx)) + _mod_rows(sh_ref, is_ctx)).astype(BF16)
    lg_ref[...] = jnp.dot(n2, wr_ref[...].astype(BF16), preferred_element_type=F32)


def norm_logits(h, gain, mod, k_shift, k_scale, w_router, li, t_lat):
    rows, d = h.shape
    e = w_router.shape[2]
    return pl.pallas_call(
        functools.partial(_norm_logits_kernel, n_lat_tiles=t_lat // ROW_TILE),
        grid=(rows // ROW_TILE,),
        in_specs=[pl.BlockSpec((ROW_TILE, d), lambda i: (i, 0)),
                  pl.BlockSpec((1, d), lambda i: (0, 0)),
                  pl.BlockSpec((8, d), lambda i: (0, k_shift)),
                  pl.BlockSpec((8, d), lambda i: (0, k_scale)),
                  pl.BlockSpec((None, d, e), lambda i: (li, 0, 0))],
        out_specs=pl.BlockSpec((ROW_TILE, e), lambda i: (i, 0)),
        out_shape=jax.ShapeDtypeStruct((rows, e), F32),
        compiler_params=_params("arbitrary"),
        name="norm_logits",
    )(h, gain.reshape(1, d), mod, mod, w_router)


def _route_kernel(lg_ref, idx_ref, gate_ref, bits_ref, key_ref, ahi_ref, amid_ref, alo_ref, tmat_ref, *, cap):
    t, e = lg_ref.shape
    pb = min(t, 256)
    tb = min(t, 1024)
    lg = lg_ref[...]
    ex = jnp.exp(lg - jnp.max(lg, axis=-1, keepdims=True))
    aff = ex / jnp.sum(ex, axis=-1, keepdims=True)
    bits_ref[...] = lax.bitcast_convert_type(aff, I32)
    ahi = aff.astype(BF16)
    r1 = aff - ahi.astype(F32)
    amid = r1.astype(BF16)
    ahi_ref[...] = ahi
    amid_ref[...] = amid
    alo_ref[...] = (r1 - amid.astype(F32)).astype(BF16)
    tok = lax.broadcasted_iota(I32, (t, e), 0)
    lane = lax.broadcasted_iota(I32, (t, e), 1)
    tmat_ref[...] = jnp.where(lane == 0, tok >> 7, jnp.where(lane == 1, tok & 127, 0)).astype(F32).astype(BF16)

    def thr_body(it, thr):
        cand = thr | (jnp.int32(1) << (30 - it))
        cnt = jnp.sum(jnp.where(bits_ref[...] >= cand, 1.0, 0.0), axis=0, keepdims=True)
        return jnp.where(cnt >= cap, cand, thr)

    thr = lax.fori_loop(0, 31, thr_body, jnp.zeros((1, e), I32))
    n_gt = jnp.sum(jnp.where(bits_ref[...] > thr, 1.0, 0.0), axis=0, keepdims=True)
    need = cap - n_gt

    ri = lax.broadcasted_iota(I32, (pb, pb), 0)
    ci = lax.broadcasted_iota(I32, (pb, pb), 1)
    ltri = jnp.where(ci < ri, 1.0, 0.0).astype(BF16)

    def pre_body(b, carry):
        ceq, csel = carry
        rows = pl.ds(pl.multiple_of(b * pb, pb), pb)
        bt = bits_ref[rows, :]
        gtb = bt > thr
        eqf = jnp.where(bt == thr, 1.0, 0.0)
        eq_rank = jnp.dot(ltri, eqf.astype(BF16), preferred_element_type=F32) + ceq
        self_ = jnp.where(gtb, 1.0, jnp.where(eq_rank < need, eqf, 0.0))
        pos = jnp.dot(ltri, self_.astype(BF16), preferred_element_type=F32) + csel
        key_ref[rows, :] = jnp.where(self_ > 0.0, pos, -1.0).astype(I32)
        return (ceq + jnp.sum(eqf, axis=0, keepdims=True), csel + jnp.sum(self_, axis=0, keepdims=True))

    zero = jnp.zeros((1, e), F32)
    lax.fori_loop(0, t // pb, pre_body, (zero, zero))

    slot = lax.broadcasted_iota(I32, (1, cap), 1)
    dn_t = (((0,), (0,)), ((), ()))
    for ex_i in range(e):
        def slot_body(b, carry, ex_i=ex_i):
            acc_g, acc_t = carry
            rows = pl.ds(pl.multiple_of(b * tb, tb), tb)
            onehot = jnp.where(key_ref[rows, ex_i:ex_i + 1] == slot, 1.0, 0.0).astype(BF16)
            part = (lax.dot_general(ahi_ref[rows, :], onehot, dn_t, preferred_element_type=F32)
                    + lax.dot_general(amid_ref[rows, :], onehot, dn_t, preferred_element_type=F32)
                    + lax.dot_general(alo_ref[rows, :], onehot, dn_t, preferred_element_type=F32))
            tpart = lax.dot_general(tmat_ref[rows, :], onehot, dn_t, preferred_element_type=F32)
            return acc_g + part, acc_t + tpart

        z = jnp.zeros((e, cap), F32)
        acc_g, acc_t = lax.fori_loop(0, t // tb, slot_body, (z, z))
        gate_ref[ex_i:ex_i + 1, :] = acc_g[ex_i:ex_i + 1, :]
        idx_ref[ex_i:ex_i + 1, :] = (acc_t[0:1, :] * 128.0 + acc_t[1:2, :]).astype(I32)


def route(logits, cap):
    t, e = logits.shape
    return pl.pallas_call(
        functools.partial(_route_kernel, cap=cap),
        out_shape=(jax.ShapeDtypeStruct((e, cap), I32), jax.ShapeDtypeStruct((e, cap), F32)),
        scratch_shapes=[pltpu.VMEM((t, e), I32), pltpu.VMEM((t, e), I32), pltpu.VMEM((t, e), BF16),
                        pltpu.VMEM((t, e), BF16), pltpu.VMEM((t, e), BF16), pltpu.VMEM((t, e), BF16)],
        compiler_params=pltpu.CompilerParams(vmem_limit_bytes=VMEM_LIMIT),
        name="route",
    )(logits)


def _issue_rows(n, fn):
    lax.fori_loop(0, n, lambda r, carry: (fn(r), carry)[1], 0, unroll=DMA_UNROLL)


def _gather_kernel(idx_sm, h_hbm, gain_ref, sh_ref, sc_ref, o_ref, buf, rinv_ref, mul_ref, sh_b_ref, sems,
                   *, sb, n_lat_slots):
    ex_i = pl.program_id(0)
    n_sub = buf.shape[0] // sb

    def row_copy(r, tok, j):
        return pltpu.make_async_copy(h_hbm.at[pl.ds(tok, 1), :], buf.at[pl.ds(r, 1), :], sems.at[j])

    def start_sub(j, carry):
        _issue_rows(sb, lambda r: row_copy(j * sb + r, idx_sm[ex_i, j * sb + r], j).start())
        return carry

    lax.fori_loop(0, n_sub, start_sub, 0)
    for kind in range(2):
        mul_ref[kind] = jnp.broadcast_to(gain_ref[...] * (1.0 + sc_ref[kind:kind + 1, :]), mul_ref.shape[1:])
        sh_b_ref[kind] = jnp.broadcast_to(sh_ref[kind:kind + 1, :], sh_b_ref.shape[1:])

    def finish_sub(j, carry):
        _issue_rows(sb, lambda r: row_copy(j * sb + r, 0, j).wait())
        xs = buf[pl.ds(pl.multiple_of(j * sb, sb), sb), :]
        rinv_ref[...] = lax.rsqrt(jnp.mean(xs * xs, axis=-1, keepdims=True) + NORM_EPS)

        def group(g, c2):
            r0 = pl.multiple_of(j * sb + g * ROW_GROUP, ROW_GROUP)
            rows = pl.ds(r0, ROW_GROUP)
            kind = (r0 >= n_lat_slots).astype(I32)
            y = buf[rows, :] * rinv_ref[pl.ds(pl.multiple_of(g * ROW_GROUP, ROW_GROUP), ROW_GROUP), :]
            o_ref[rows, :] = (y * mul_ref[kind] + sh_b_ref[kind]).astype(o_ref.dtype)
            return c2

        lax.fori_loop(0, sb // ROW_GROUP, group, 0, unroll=True)
        return carry

    lax.fori_loop(0, n_sub, finish_sub, 0)


def moe_gather(h, idx, gain, mod, k_shift, k_scale, n_lat_slots, sb):
    e, ct = idx.shape
    d = h.shape[1]
    assert n_lat_slots % ROW_GROUP == 0
    grid_spec = pltpu.PrefetchScalarGridSpec(
        num_scalar_prefetch=1,
        grid=(e,),
        in_specs=[pl.BlockSpec(memory_space=pl.ANY),
                  pl.BlockSpec((1, d), lambda i, idx_sm: (0, 0)),
                  pl.BlockSpec((8, d), lambda i, idx_sm: (0, k_shift)),
                  pl.BlockSpec((8, d), lambda i, idx_sm: (0, k_scale))],
        out_specs=pl.BlockSpec((None, ct, d), lambda i, idx_sm: (i, 0, 0)),
        scratch_shapes=[pltpu.VMEM((ct, d), F32), pltpu.VMEM((sb, 1), F32),
                        pltpu.VMEM((2, ROW_GROUP, d), F32), pltpu.VMEM((2, ROW_GROUP, d), F32),
                        pltpu.SemaphoreType.DMA((ct // sb,))],
    )
    return pl.pallas_call(
        functools.partial(_gather_kernel, sb=sb, n_lat_slots=n_lat_slots),
        grid_spec=grid_spec,
        out_shape=jax.ShapeDtypeStruct((e, ct, d), BF16),
        compiler_params=_params("arbitrary"),
        name="moe_gather",
    )(idx, h, gain.reshape(1, d), mod, mod)


def _ffn_up_kernel(x_ref, wg_ref, wu_ref, o_ref):
    x = x_ref[...]
    g = jnp.dot(x, wg_ref[...].astype(BF16), preferred_element_type=F32)
    u = jnp.dot(x, wu_ref[...].astype(BF16), preferred_element_type=F32)
    o_ref[...] = (_silu(g) * u).astype(o_ref.dtype)


def ffn_up(xg, w_gate, w_up, li):
    e, ct, d = xg.shape
    f = w_gate.shape[3]
    fb = 256
    return pl.pallas_call(
        _ffn_up_kernel,
        grid=(e, f // fb),
        in_specs=[pl.BlockSpec((None, ct, d), lambda i, j: (i, 0, 0)),
                  pl.BlockSpec((None, None, d, fb), lambda i, j: (li, i, 0, j)),
                  pl.BlockSpec((None, None, d, fb), lambda i, j: (li, i, 0, j))],
        out_specs=pl.BlockSpec((None, ct, fb), lambda i, j: (i, 0, j)),
        out_shape=jax.ShapeDtypeStruct((e, ct, f), BF16),
        compiler_params=_params("arbitrary", "arbitrary"),
        name="ffn_up",
    )(xg, w_gate, w_up)


def _ffn_down_kernel(h_ref, wd_ref, gate_ref, g2_ref, o_ref, *, n_lat_slots):
    acc = jnp.dot(h_ref[...], wd_ref[...].astype(BF16), preferred_element_type=F32)
    is_ctx = lax.broadcasted_iota(I32, (acc.shape[0], 1), 0) >= n_lat_slots
    g2 = jnp.where(is_ctx, g2_ref[1:2, :], g2_ref[0:1, :])
    o_ref[...] = (acc * gate_ref[...] * g2).astype(o_ref.dtype)


def ffn_down(hid, w_down, li, gate, mod, k_gate, n_lat_slots):
    e, ct, f = hid.shape
    d = w_down.shape[3]
    nb = _largest_divisor(d, (1024, 512, 256, 128))
    nblk = d // nb
    return pl.pallas_call(
        functools.partial(_ffn_down_kernel, n_lat_slots=n_lat_slots),
        grid=(e, nblk),
        in_specs=[pl.BlockSpec((None, ct, f), lambda i, j: (i, 0, 0)),
                  pl.BlockSpec((None, None, f, nb), lambda i, j: (li, i, 0, j)),
                  pl.BlockSpec((None, ct, 1), lambda i, j: (i, 0, 0)),
                  pl.BlockSpec((8, nb), lambda i, j: (0, k_gate * nblk + j))],
        out_specs=pl.BlockSpec((None, ct, nb), lambda i, j: (i, 0, j)),
        out_shape=jax.ShapeDtypeStruct((e, ct, d), BF16),
        compiler_params=_params("arbitrary", "arbitrary"),
        name="ffn_down",
    )(hid, w_down, gate[:, :, None], mod)


def _scatter_kernel(idx_sm, y_ref, acc_in, acc_out, buf, sems_in, sem_out, *, sb):
    del acc_in
    ex_i = pl.program_id(0)
    ct = buf.shape[0]
    n_sub = ct // sb

    def fetch(r, tok, j):
        return pltpu.make_async_copy(acc_out.at[pl.ds(tok, 1), :], buf.at[pl.ds(r, 1), :], sems_in.at[j])

    def put(r, tok):
        return pltpu.make_async_copy(buf.at[pl.ds(r, 1), :], acc_out.at[pl.ds(tok, 1), :], sem_out)

    def start_sub(j, carry):
        _issue_rows(sb, lambda r: fetch(j * sb + r, idx_sm[ex_i, j * sb + r], j).start())
        return carry

    lax.fori_loop(0, n_sub, start_sub, 0)

    def finish_sub(j, carry):
        _issue_rows(sb, lambda r: fetch(j * sb + r, 0, j).wait())
        rows = pl.ds(pl.multiple_of(j * sb, sb), sb)
        buf[rows, :] = buf[rows, :] + y_ref[rows, :].astype(F32)
        _issue_rows(sb, lambda r: put(j * sb + r, idx_sm[ex_i, j * sb + r]).start())
        return carry

    lax.fori_loop(0, n_sub, finish_sub, 0)
    _issue_rows(ct, lambda r: put(r, 0).wait())


def moe_scatter_add(acc, idx, y, sb):
    e, ct = idx.shape
    rows, d = acc.shape
    grid_spec = pltpu.PrefetchScalarGridSpec(
        num_scalar_prefetch=1,
        grid=(e,),
        in_specs=[pl.BlockSpec((None, ct, d), lambda i, idx_sm: (i, 0, 0)),
                  pl.BlockSpec(memory_space=pl.ANY)],
        out_specs=pl.BlockSpec(memory_space=pl.ANY),
        scratch_shapes=[pltpu.VMEM((ct, d), F32), pltpu.SemaphoreType.DMA((ct // sb,)),
                        pltpu.SemaphoreType.DMA(())],
    )
    return pl.pallas_call(
        functools.partial(_scatter_kernel, sb=sb),
        grid_spec=grid_spec,
        out_shape=jax.ShapeDtypeStruct((rows, d), F32),
        input_output_aliases={2: 0},
        compiler_params=_params("arbitrary"),
        name="moe_scatter_add",
    )(idx, y, acc)


def expert_choice_moe(h, gain, mod, w_router, w_gate, w_up, w_down, li, groups, t_lat):
    e = w_router.shape[2]
    logits = norm_logits(h, gain, mod, 3, 4, w_router, li, t_lat)
    idx_parts, gate_parts = [], []
    for start, count in groups:
        idx_g, gate_g = route(logits[start:start + count], EC_CAPACITY_FACTOR * count // e)
        idx_parts.append(idx_g + start)
        gate_parts.append(gate_g)
    idx = jnp.concatenate(idx_parts, axis=1)
    gate = jnp.concatenate(gate_parts, axis=1)
    ct = idx.shape[1]
    sb = _largest_divisor(ct, (128, 96, 64, 32, 16))
    n_lat_slots = EC_CAPACITY_FACTOR * t_lat // e
    xg = moe_gather(h, idx, gain, mod, 3, 4, n_lat_slots, sb)
    hid = ffn_up(xg, w_gate, w_up, li)
    y = ffn_down(hid, w_down, li, gate, mod, 5, n_lat_slots)
    return moe_scatter_add(h, idx, y, sb)


def _axial_tables(t_lat, t_ctx, rot_dim):
    axis_dim = rot_dim // 2
    tok = jnp.arange(t_lat)
    row = (tok // GRID_W).astype(F32)
    col = (tok % GRID_W).astype(F32)
    inv_freq = ROPE_BASE ** (-jnp.arange(0, axis_dim, 2, dtype=F32) / axis_dim)
    ar, ac = row[:, None] * inv_freq, col[:, None] * inv_freq
    cos = jnp.concatenate([jnp.cos(ar), jnp.cos(ar), jnp.cos(ac), jnp.cos(ac)], axis=-1)
    sin = jnp.concatenate([-jnp.sin(ar), jnp.sin(ar), -jnp.sin(ac), jnp.sin(ac)], axis=-1)
    cos = jnp.concatenate([cos, jnp.ones((t_ctx, rot_dim), F32)], axis=0)
    sin = jnp.concatenate([sin, jnp.zeros((t_ctx, rot_dim), F32)], axis=0)
    return cos, sin


def kernel(x, c, ctx, c_ctx, ada_w, ada_b, norm_mix, norm_ffn, diff_w_in, diff_w_out, diff_q_norm, diff_k_norm, diff_lambda_q1, diff_lambda_k1, diff_lambda_q2, diff_lambda_k2, diff_subln, ret_w_in, ret_w_out, ret_decay_fwd, ret_decay_bwd, router_w, expert_w_gate, expert_w_up, expert_w_down):
    batch, t_lat, d = x.shape
    t_ctx = ctx.shape[1]
    depth = ada_w.shape[0]
    assert batch == 1 and depth == 2 and t_lat % ROW_TILE == 0 and t_ctx == ROW_TILE
    t_all = t_lat + t_ctx
    diff_heads = d // DIFF_V_DIM
    ret_heads = d // RET_QK_DIM

    h = jnp.concatenate([x[0], ctx[0]], axis=0)
    cvec = jnp.zeros((8, d), F32).at[0].set(c[0]).at[1].set(c_ctx)
    cos_d, sin_d = _axial_tables(t_lat, t_ctx, DIFF_HEAD_DIM)
    cos_r, sin_r = _axial_tables(t_lat, t_ctx, RET_QK_DIM)

    mod = adaln(cvec, ada_w, ada_b, 0)
    n1 = norm_mod(h, norm_mix[0], mod, 0, 1, t_lat)
    qkv = mm_diff_in(n1, diff_w_in, 0, diff_q_norm[0], diff_k_norm[0], cos_d, sin_d)
    lam_vecs = (diff_lambda_q1[0], diff_lambda_k1[0], diff_lambda_q2[0], diff_lambda_k2[0])
    lambda_init = 0.8 - 0.6 * math.exp(-0.3 * 0)
    bq = _largest_divisor(t_lat, (512, 256))
    kc = _largest_divisor(t_all, (768, 256))
    att_lat = diff_attn(qkv, lam_vecs, diff_subln[0], diff_heads, (0, t_lat), (0, t_all), bq, kc, lambda_init)
    att_ctx = diff_attn(qkv, lam_vecs, diff_subln[0], diff_heads, (t_lat, t_ctx), (t_lat, t_ctx), t_ctx, t_ctx,
                        lambda_init)
    h = mm_resid(jnp.concatenate([att_lat, att_ctx], axis=0), diff_w_out, 0, h, mod, 2, t_lat)
    h = expert_choice_moe(h, norm_ffn[0], mod, router_w, expert_w_gate, expert_w_up, expert_w_down, 0,
                          [(0, t_lat), (t_lat, t_ctx)], t_lat)

    mod = adaln(cvec, ada_w, ada_b, 1)
    n1 = norm_mod(h, norm_mix[1], mod, 0, 1, t_lat)
    proj = mm_ret_in(n1, ret_w_in, 0, cos_r, sin_r)
    gated = retention_dir(proj, ret_decay_fwd[0], ret_heads, t_lat, backward=False)
    gated = retention_dir(proj, ret_decay_bwd[0], ret_heads, t_lat, backward=True, prev=gated)
    h = mm_resid(gated, ret_w_out, 0, h, mod, 2, t_lat, single_buffer_w=True)
    h = expert_choice_moe(h, norm_ffn[1], mod, router_w, expert_w_gate, expert_w_up, expert_w_down, 1,
                          [(0, t_lat)], t_lat)
    return h[None]
```

```python
import functools
import math

import jax
import jax.numpy as jnp
from jax import lax
from jax.experimental import pallas as pl
from jax.experimental.pallas import tpu as pltpu

F32 = jnp.float32
BF16 = jnp.bfloat16
I32 = jnp.int32

NORM_EPS = 1e-6
GRID_W = 64
ROPE_BASE = 10000.0
DIFF_HEAD_DIM = 128
DIFF_V_DIM = 2 * DIFF_HEAD_DIM
RET_QK_DIM = 256
RET_V_DIM = 2 * RET_QK_DIM
RET_CHUNK = 256
EC_CAPACITY_FACTOR = 2
ROW_TILE = 256
ROW_GROUP = 16
DMA_UNROLL = 8
MM_PART_ROWS = (256, 352)
LANES = 128
V7X_VMEM_BYTES = 64 * 1024 * 1024
VMEM_LIMIT = V7X_VMEM_BYTES - 8 * 1024 * 1024


def _params(*sem):
    return pltpu.CompilerParams(dimension_semantics=sem, vmem_limit_bytes=VMEM_LIMIT)


def _silu(x):
    return x * (1.0 / (1.0 + jnp.exp(-x)))


def _rms(x):
    return x * lax.rsqrt(jnp.mean(x * x, axis=-1, keepdims=True) + NORM_EPS)


def _largest_divisor(n, candidates):
    for cand in candidates:
        if n % cand == 0:
            return cand
    raise ValueError(f"no tile in {candidates} divides {n}")


def _mm_tiles(m, k, f32_rows=False):
    bn = 512 if k <= 4096 else 256
    if k > 4096:
        rows = (512, 256)
    elif f32_rows:
        rows = (1024, 768, 512, 256)
    else:
        rows = (1408, 1024, 768, 512, 256)
    return _largest_divisor(m, rows), bn


def _adaln_kernel(c_ref, w_ref, b_ref, o_ref):
    s = _silu(c_ref[...])
    o_ref[...] = jnp.dot(s.astype(BF16), w_ref[...].astype(BF16), preferred_element_type=F32) + b_ref[...]


def adaln(cvec, w, b, li):
    _, d, n = w.shape
    bn = 512
    return pl.pallas_call(
        _adaln_kernel,
        grid=(n // bn,),
        in_specs=[pl.BlockSpec((8, d), lambda j: (0, 0)),
                  pl.BlockSpec((None, d, bn), lambda j: (li, 0, j)),
                  pl.BlockSpec((None, 1, bn), lambda j: (li, 0, j))],
        out_specs=pl.BlockSpec((8, bn), lambda j: (0, j)),
        out_shape=jax.ShapeDtypeStruct((8, n), F32),
        compiler_params=_params("arbitrary"),
        name="adaln",
    )(cvec, w, b[:, None, :])


def _mod_rows(mod_ref, is_ctx):
    return jnp.where(is_ctx, mod_ref[1:2, :], mod_ref[0:1, :])


def _norm_mod_kernel(h_ref, gain_ref, sh_ref, sc_ref, o_ref, *, n_lat_tiles):
    is_ctx = pl.program_id(0) >= n_lat_tiles
    y = _rms(h_ref[...]) * gain_ref[...]
    o_ref[...] = (y * (1.0 + _mod_rows(sc_ref, is_ctx)) + _mod_rows(sh_ref, is_ctx)).astype(o_ref.dtype)


def norm_mod(h, gain, mod, k_shift, k_scale, t_lat):
    rows, d = h.shape
    return pl.pallas_call(
        functools.partial(_norm_mod_kernel, n_lat_tiles=t_lat // ROW_TILE),
        grid=(rows // ROW_TILE,),
        in_specs=[pl.BlockSpec((ROW_TILE, d), lambda i: (i, 0)),
                  pl.BlockSpec((1, d), lambda i: (0, 0)),
                  pl.BlockSpec((8, d), lambda i: (0, k_shift)),
                  pl.BlockSpec((8, d), lambda i: (0, k_scale))],
        out_specs=pl.BlockSpec((ROW_TILE, d), lambda i: (i, 0)),
        out_shape=jax.ShapeDtypeStruct((rows, d), BF16),
        compiler_params=_params("arbitrary"),
        name="norm_mod",
    )(h, gain.reshape(1, d), mod, mod)


def _mm_parts(a_ref, w_ref, wbf_ref):
    @pl.when(pl.program_id(1) == 0)
    def _():
        wbf_ref[...] = w_ref[...].astype(BF16)

    bm = a_ref.shape[0]
    part = _largest_divisor(bm, MM_PART_ROWS)
    for r in range(bm // part):
        rows = slice(r * part, (r + 1) * part)
        yield rows, jnp.dot(a_ref[rows, :], wbf_ref[...], preferred_element_type=F32)


def _rot_half_64(x):
    lane = lax.broadcasted_iota(I32, (1, LANES), 1)
    first = (lane % 64) < 32
    return jnp.where(first, pltpu.roll(x, 96, 1), pltpu.roll(x, 32, 1))


def _mm_diff_in_kernel(a_ref, w_ref, qg_ref, kg_ref, cos_ref, sin_ref, o_ref, wbf_ref, *, nq, nqk, qscale):
    j = pl.program_id(0)
    is_q, is_qk = j < nq, j < nqk
    gain = jnp.where(is_q, qg_ref[...], kg_ref[...])
    scale = jnp.where(is_q, qscale, 1.0)
    for rows, acc in _mm_parts(a_ref, w_ref, wbf_ref):
        cos, sin = cos_ref[rows, :], sin_ref[rows, :]
        for cidx in range(acc.shape[1] // LANES):
            sl = slice(cidx * LANES, (cidx + 1) * LANES)
            x = acc[:, sl]
            y = _rms(x) * gain
            y = (y * cos + _rot_half_64(y) * sin) * scale
            o_ref[rows, sl] = jnp.where(is_qk, y, x).astype(o_ref.dtype)


def mm_diff_in(a, w, li, q_gain, k_gain, cos, sin):
    m, k = a.shape
    n = w.shape[2]
    d = n // 3
    bm, bn = _mm_tiles(m, k)
    kern = functools.partial(_mm_diff_in_kernel, nq=d // bn, nqk=2 * d // bn,
                             qscale=DIFF_HEAD_DIM ** -0.5 * math.log2(math.e))
    return pl.pallas_call(
        kern,
        grid=(n // bn, m // bm),
        in_specs=[pl.BlockSpec((bm, k), lambda j, i: (i, 0)),
                  pl.BlockSpec((None, k, bn), lambda j, i: (li, 0, j)),
                  pl.BlockSpec((1, LANES), lambda j, i: (0, 0)),
                  pl.BlockSpec((1, LANES), lambda j, i: (0, 0)),
                  pl.BlockSpec((bm, LANES), lambda j, i: (i, 0)),
                  pl.BlockSpec((bm, LANES), lambda j, i: (i, 0))],
        out_specs=pl.BlockSpec((bm, bn), lambda j, i: (i, j)),
        out_shape=jax.ShapeDtypeStruct((m, n), BF16),
        scratch_shapes=[pltpu.VMEM((k, bn), BF16)],
        compiler_params=_params("arbitrary", "arbitrary"),
        name="mm_diff_in",
    )(a, w, q_gain.reshape(1, LANES), k_gain.reshape(1, LANES), cos, sin)


def _mm_ret_in_kernel(a_ref, w_ref, cos_ref, sin_ref, o_ref, wbf_ref, *, nq, nqk, kscale):
    j = pl.program_id(0)
    is_qk = j < nqk
    scale = jnp.where(j < nq, 1.0, kscale)
    for rows, acc in _mm_parts(a_ref, w_ref, wbf_ref):
        for cidx in range(acc.shape[1] // LANES):
            sl = slice(cidx * LANES, (cidx + 1) * LANES)
            tl = slice((cidx % 2) * LANES, (cidx % 2 + 1) * LANES)
            x = acc[:, sl]
            y = (x * cos_ref[rows, tl] + pltpu.roll(x, 64, 1) * sin_ref[rows, tl]) * scale
            o_ref[rows, sl] = jnp.where(is_qk, y, x).astype(o_ref.dtype)


def mm_ret_in(a, w, li, cos, sin):
    m, k = a.shape
    n = w.shape[2]
    d = n // 8
    bm, bn = _mm_tiles(m, k)
    kern = functools.partial(_mm_ret_in_kernel, nq=d // bn, nqk=2 * d // bn, kscale=RET_QK_DIM ** -0.5)
    return pl.pallas_call(
        kern,
        grid=(n // bn, m // bm),
        in_specs=[pl.BlockSpec((bm, k), lambda j, i: (i, 0)),
                  pl.BlockSpec((None, k, bn), lambda j, i: (li, 0, j)),
                  pl.BlockSpec((bm, RET_QK_DIM), lambda j, i: (i, 0)),
                  pl.BlockSpec((bm, RET_QK_DIM), lambda j, i: (i, 0))],
        out_specs=pl.BlockSpec((bm, bn), lambda j, i: (i, j)),
        out_shape=jax.ShapeDtypeStruct((m, n), BF16),
        scratch_shapes=[pltpu.VMEM((k, bn), BF16)],
        compiler_params=_params("arbitrary", "arbitrary"),
        name="mm_ret_in",
    )(a, w, cos, sin)


def _mm_resid_kernel(a_ref, w_ref, h_ref, g_ref, o_ref, wbf_ref, *, t_lat):
    bm = a_ref.shape[0]
    for rows, acc in _mm_parts(a_ref, w_ref, wbf_ref):
        row = pl.program_id(1) * bm + rows.start + lax.broadcasted_iota(I32, (acc.shape[0], 1), 0)
        gate = jnp.where(row >= t_lat, g_ref[1:2, :], g_ref[0:1, :])
        o_ref[rows, :] = h_ref[rows, :] + gate * acc


def mm_resid(a, w, li, h, mod, k_gate, t_lat, single_buffer_w=False):
    m, k = a.shape
    n = w.shape[2]
    bm, bn = _mm_tiles(m, k, f32_rows=True)
    if single_buffer_w:
        bn = 512
        w_spec = pl.BlockSpec((None, k, bn), lambda j, i: (li, 0, j), pipeline_mode=pl.Buffered(1))
    else:
        w_spec = pl.BlockSpec((None, k, bn), lambda j, i: (li, 0, j))
    nb = n // bn
    return pl.pallas_call(
        functools.partial(_mm_resid_kernel, t_lat=t_lat),
        grid=(nb, m // bm),
        in_specs=[pl.BlockSpec((bm, k), lambda j, i: (i, 0)),
                  w_spec,
                  pl.BlockSpec((bm, bn), lambda j, i: (i, j)),
                  pl.BlockSpec((8, bn), lambda j, i: (0, k_gate * nb + j))],
        out_specs=pl.BlockSpec((bm, bn), lambda j, i: (i, j)),
        out_shape=jax.ShapeDtypeStruct((m, n), F32),
        scratch_shapes=[pltpu.VMEM((k, bn), BF16)],
        compiler_params=_params("arbitrary", "arbitrary"),
        name="mm_resid",
    )(a, w, h, mod)


def _diff_attn_kernel(lq1_ref, lk1_ref, lq2_ref, lk2_ref, sub_ref, q_ref, k_ref, v_ref,
                      o_ref, s_a, s_b, m1, l1, a1, m2, l2, a2, *, kc, lambda_init):
    hd = DIFF_HEAD_DIM
    n_chunks = k_ref.shape[0] // kc
    maps = ((m1, l1, a1), (m2, l2, a2))
    for m_ref, l_ref, a_ref in maps:
        m_ref[...] = jnp.full(m_ref.shape, -jnp.inf, F32)
        l_ref[...] = jnp.zeros(l_ref.shape, F32)
        a_ref[...] = jnp.zeros(a_ref.shape, F32)
    dn = (((1,), (1,)), ((), ()))

    def chunk_rows(c):
        return pl.ds(pl.multiple_of(c * kc, kc), kc)

    def scores(c, s_ref):
        rows = chunk_rows(c)
        for half in range(2):
            cols = slice(half * hd, (half + 1) * hd)
            s_ref[half] = lax.dot_general(q_ref[:, cols], k_ref[rows, cols], dn, preferred_element_type=F32)

    def update(c, s_ref):
        vs = v_ref[chunk_rows(c), :]
        for half, (m_ref, l_ref, a_ref) in enumerate(maps):
            s = s_ref[half]
            m_old = m_ref[...]
            m_new = jnp.maximum(m_old, jnp.max(s, axis=-1, keepdims=True))
            alpha = jnp.exp2(m_old - m_new)
            p = jnp.exp2(s - m_new)
            l_ref[...] = alpha * l_ref[...] + jnp.sum(p, axis=-1, keepdims=True)
            a_ref[...] = alpha * a_ref[...] + jnp.dot(p.astype(BF16), vs, preferred_element_type=F32)
            m_ref[...] = m_new

    scores(0, s_a)
    n_pairs = (n_chunks - 1) // 2

    def pair(i, carry):
        c = 2 * i
        scores(c + 1, s_b)
        update(c, s_a)
        scores(c + 2, s_a)
        update(c + 1, s_b)
        return carry

    lax.fori_loop(0, n_pairs, pair, 0)
    if n_chunks % 2 == 0:
        scores(n_chunks - 1, s_b)
        update(n_chunks - 2, s_a)
        update(n_chunks - 1, s_b)
    else:
        update(n_chunks - 1, s_a)

    lam = (jnp.exp(jnp.sum(lq1_ref[...] * lk1_ref[...], keepdims=True))
           - jnp.exp(jnp.sum(lq2_ref[...] * lk2_ref[...], keepdims=True))) + lambda_init
    o = a1[...] / l1[...] - lam * (a2[...] / l2[...])
    o_ref[...] = (_rms(o) * sub_ref[...] * (1.0 - lambda_init)).astype(o_ref.dtype)


def diff_attn(qkv, lam_vecs, subln, n_heads, q_rows, kv_rows, bq, kc, lambda_init):
    d = n_heads * DIFF_V_DIM
    q0, nq = q_rows
    k0, nk = kv_rows
    assert q0 % bq == 0 and nq % bq == 0 and k0 % nk == 0 and nk % kc == 0
    qb0, kb0 = q0 // bq, k0 // nk
    vec = pl.BlockSpec((1, LANES), lambda h, i: (0, 0))
    in_specs = [vec, vec, vec, vec,
                pl.BlockSpec((1, DIFF_V_DIM), lambda h, i: (0, 0)),
                pl.BlockSpec((bq, DIFF_V_DIM), lambda h, i: (qb0 + i, h)),
                pl.BlockSpec((nk, DIFF_V_DIM), lambda h, i: (kb0, n_heads + h)),
                pl.BlockSpec((nk, DIFF_V_DIM), lambda h, i: (kb0, 2 * n_heads + h))]
    args = [v.reshape(1, LANES) for v in lam_vecs] + [subln.reshape(1, DIFF_V_DIM), qkv, qkv, qkv]
    kern = functools.partial(_diff_attn_kernel, kc=kc, lambda_init=lambda_init)
    return pl.pallas_call(
        kern,
        grid=(n_heads, nq // bq),
        in_specs=in_specs,
        out_specs=pl.BlockSpec((bq, DIFF_V_DIM), lambda h, i: (i, h)),
        out_shape=jax.ShapeDtypeStruct((nq, d), BF16),
        scratch_shapes=[pltpu.VMEM((2, bq, kc), F32), pltpu.VMEM((2, bq, kc), F32),
                        pltpu.VMEM((bq, 1), F32), pltpu.VMEM((bq, 1), F32), pltpu.VMEM((bq, DIFF_V_DIM), F32),
                        pltpu.VMEM((bq, 1), F32), pltpu.VMEM((bq, 1), F32), pltpu.VMEM((bq, DIFF_V_DIM), F32)],
        compiler_params=_params("arbitrary", "arbitrary"),
        name="diff_attn",
    )(*args)


def _retention_kernel(dec_ref, q_ref, k_ref, v_ref, g_ref, kc_ref, vc_ref, *rest, backward, has_prev):
    if has_prev:
        prev_ref, o_ref, s_ref, intra_ref, qd_ref, kd_ref, cd_ref = rest
    else:
        prev_ref = None
        o_ref, s_ref, intra_ref, qd_ref, kd_ref, cd_ref = rest
    c = RET_CHUNK
    n_chunks = q_ref.shape[0] // c
    dn_t = (((0,), (0,)), ((), ()))
    dn_nt = (((1,), (1,)), ((), ()))

    @pl.when(pl.program_id(1) == 0)
    def _():
        lg = jnp.log1p(-jnp.exp(dec_ref[0]))[0:1, 0:1]
        i = lax.broadcasted_iota(I32, (c, c), 0)
        j = lax.broadcasted_iota(I32, (c, c), 1)
        pos = lax.broadcasted_iota(I32, (c, 1), 0).astype(F32)
        rel = (j - i) if backward else (i - j)
        relf = jnp.maximum(rel, 0).astype(F32)
        intra_ref[...] = jnp.where(rel >= 0, jnp.exp(lg * relf), 0.0)
        if backward:
            qd_ref[...] = jnp.exp(lg * (c - pos))
            kd_ref[...] = jnp.exp(lg * pos)
        else:
            qd_ref[...] = jnp.exp(lg * (pos + 1.0))
            kd_ref[...] = jnp.exp(lg * (c - 1.0 - pos))
        cd_ref[...] = jnp.broadcast_to(jnp.exp(lg * float(c)), cd_ref.shape)
        kd = (kc_ref[...].astype(F32) * kd_ref[...]).astype(BF16)
        s_ref[...] = lax.dot_general(kd, vc_ref[...], dn_t, preferred_element_type=F32)

    def body(t, carry):
        cc = (n_chunks - 1 - t) if backward else t
        rows = pl.ds(pl.multiple_of(cc * c, c), c)
        q, k, v = q_ref[rows, :], k_ref[rows, :], v_ref[rows, :]
        scores = lax.dot_general(q, k, dn_nt, preferred_element_type=F32) * intra_ref[...]
        s_old = s_ref[...]
        qd = (q.astype(F32) * qd_ref[...]).astype(BF16)
        o = (jnp.dot(scores.astype(BF16), v, preferred_element_type=F32)
             + jnp.dot(qd, s_old.astype(BF16), preferred_element_type=F32))
        kd = (k.astype(F32) * kd_ref[...]).astype(BF16)
        s_ref[...] = s_old * cd_ref[0:1, 0:1] + lax.dot_general(kd, v, dn_t, preferred_element_type=F32)
        out = _silu(g_ref[rows, :].astype(F32)) * _rms(o)
        if has_prev:
            out = out + prev_ref[rows, :].astype(F32)
        o_ref[rows, :] = out.astype(o_ref.dtype)
        return carry

    lax.fori_loop(0, n_chunks, body, 0, unroll=2)


def retention_dir(proj, decay, n_heads, t_lat, backward, prev=None):
    c = RET_CHUNK
    d = n_heads * RET_QK_DIM
    t_ctx = proj.shape[0] - t_lat
    assert t_ctx == c and t_lat % c == 0
    rows = _largest_divisor(t_lat, (2048, 1024, 512, 256))
    n_steps = t_lat // rows
    step = (lambda s: n_steps - 1 - s) if backward else (lambda s: s)
    qb, vb = d // RET_QK_DIM, 2 * d // RET_V_DIM
    gb = (6 if backward else 4) * d // RET_V_DIM
    ctx_blk = t_lat // c
    in_specs = [pl.BlockSpec((1, 8, LANES), lambda h, s: (h, 0, 0)),
                pl.BlockSpec((rows, RET_QK_DIM), lambda h, s: (step(s), h)),
                pl.BlockSpec((rows, RET_QK_DIM), lambda h, s: (step(s), qb + h)),
                pl.BlockSpec((rows, RET_V_DIM), lambda h, s: (step(s), vb + h)),
                pl.BlockSpec((rows, RET_V_DIM), lambda h, s: (step(s), gb + h)),
                pl.BlockSpec((c, RET_QK_DIM), lambda h, s: (ctx_blk, qb + h)),
                pl.BlockSpec((c, RET_V_DIM), lambda h, s: (ctx_blk, vb + h))]
    args = [jnp.broadcast_to(decay.astype(F32)[:, None, None], (n_heads, 8, LANES)), proj, proj, proj, proj, proj, proj]
    if prev is not None:
        in_specs.append(pl.BlockSpec((rows, RET_V_DIM), lambda h, s: (step(s), h)))
        args.append(prev)
    kern = functools.partial(_retention_kernel, backward=backward, has_prev=prev is not None)
    return pl.pallas_call(
        kern,
        grid=(n_heads, n_steps),
        in_specs=in_specs,
        out_specs=pl.BlockSpec((rows, RET_V_DIM), lambda h, s: (step(s), h)),
        out_shape=jax.ShapeDtypeStruct((t_lat, n_heads * RET_V_DIM), BF16),
        scratch_shapes=[pltpu.VMEM((RET_QK_DIM, RET_V_DIM), F32), pltpu.VMEM((c, c), F32),
                        pltpu.VMEM((c, 1), F32), pltpu.VMEM((c, 1), F32), pltpu.VMEM((8, LANES), F32)],
        compiler_params=_params("arbitrary", "arbitrary"),
        name="retention_bwd" if backward else "retention_fwd",
    )(*args)


def _norm_logits_kernel(h_ref, gain_ref, sh_ref, sc_ref, wr_ref, lg_ref, *, n_lat_tiles):
    is_ctx = pl.program_id(0) >= n_lat_tiles
    y = _rms(h_ref[...]) * gain_ref[...]
    n2 = (y * (1.0 + _mod_rows(sc_ref, is_ctx)) + _mod_rows(sh_ref, is_ctx)).astype(BF16)
    lg_ref[...] = jnp.dot(n2, wr_ref[...].astype(BF16), preferred_element_type=F32)


def norm_logits(h, gain, mod, k_shift, k_scale, w_router, li, t_lat):
    rows, d = h.shape
    e = w_router.shape[2]
    return pl.pallas_call(
        functools.partial(_norm_logits_kernel, n_lat_tiles=t_lat // ROW_TILE),
        grid=(rows // ROW_TILE,),
        in_specs=[pl.BlockSpec((ROW_TILE, d), lambda i: (i, 0)),
                  pl.BlockSpec((1, d), lambda i: (0, 0)),
                  pl.BlockSpec((8, d), lambda i: (0, k_shift)),
                  pl.BlockSpec((8, d), lambda i: (0, k_scale)),
                  pl.BlockSpec((None, d, e), lambda i: (li, 0, 0))],
        out_specs=pl.BlockSpec((ROW_TILE, e), lambda i: (i, 0)),
        out_shape=jax.ShapeDtypeStruct((rows, e), F32),
        compiler_params=_params("arbitrary"),
        name="norm_logits",
    )(h, gain.reshape(1, d), mod, mod, w_router)


def _route_kernel(lg_ref, idx_ref, gate_ref, bits_ref, key_ref, w_ref, *, cap):
    t, e = lg_ref.shape
    pb = min(t, 256)
    tb = min(t, 1024)
    lg = lg_ref[...]
    ex = jnp.exp(lg - jnp.max(lg, axis=-1, keepdims=True))
    aff = ex / jnp.sum(ex, axis=-1, keepdims=True)
    bits_ref[...] = lax.bitcast_convert_type(aff, I32)
    ahi = aff.astype(BF16)
    r1 = aff - ahi.astype(F32)
    amid = r1.astype(BF16)
    alo = (r1 - amid.astype(F32)).astype(BF16)
    tok = lax.broadcasted_iota(I32, (t, e), 0)
    lane = lax.broadcasted_iota(I32, (t, e), 1)
    tdig = jnp.where(lane == 0, tok >> 7, jnp.where(lane == 1, tok & 127, 0)).astype(F32).astype(BF16)
    pr = lax.broadcasted_iota(I32, (e, 4 * e), 0)
    pc = lax.broadcasted_iota(I32, (e, 4 * e), 1)
    packed = jnp.zeros((t, 4 * e), F32)
    for piece, part in enumerate((ahi, amid, alo, tdig)):
        place = jnp.where(pc == pr + piece * e, 1.0, 0.0).astype(BF16)
        packed = packed + jnp.dot(part, place, preferred_element_type=F32)
    w_ref[...] = packed.astype(BF16)

    def thr_body(it, thr):
        cand = thr | (jnp.int32(1) << (30 - it))
        cnt = jnp.sum(jnp.where(bits_ref[...] >= cand, 1.0, 0.0), axis=0, keepdims=True)
        return jnp.where(cnt >= cap, cand, thr)

    thr = lax.fori_loop(0, 31, thr_body, jnp.zeros((1, e), I32))
    n_gt = jnp.sum(jnp.where(bits_ref[...] > thr, 1.0, 0.0), axis=0, keepdims=True)
    need = cap - n_gt

    ri = lax.broadcasted_iota(I32, (pb, pb), 0)
    ci = lax.broadcasted_iota(I32, (pb, pb), 1)
    ltri = jnp.where(ci < ri, 1.0, 0.0).astype(BF16)

    def pre_body(b, carry):
        ceq, csel = carry
        rows = pl.ds(pl.multiple_of(b * pb, pb), pb)
        bt = bits_ref[rows, :]
        gtb = bt > thr
        eqf = jnp.where(bt == thr, 1.0, 0.0)
        eq_rank = jnp.dot(ltri, eqf.astype(BF16), preferred_element_type=F32) + ceq
        self_ = jnp.where(gtb, 1.0, jnp.where(eq_rank < need, eqf, 0.0))
        pos = jnp.dot(ltri, self_.astype(BF16), preferred_element_type=F32) + csel
        key_ref[rows, :] = jnp.where(self_ > 0.0, pos, -1.0).astype(I32)
        return (ceq + jnp.sum(eqf, axis=0, keepdims=True), csel + jnp.sum(self_, axis=0, keepdims=True))

    zero = jnp.zeros((1, e), F32)
    lax.fori_loop(0, t // pb, pre_body, (zero, zero))

    slot = lax.broadcasted_iota(I32, (1, cap), 1)
    dn_t = (((0,), (0,)), ((), ()))
    for ex_i in range(e):
        def slot_body(b, carry, ex_i=ex_i):
            rows = pl.ds(pl.multiple_of(b * tb, tb), tb)
            onehot = jnp.where(key_ref[rows, ex_i:ex_i + 1] == slot, 1.0, 0.0).astype(BF16)
            return carry + lax.dot_general(w_ref[rows, :], onehot, dn_t, preferred_element_type=F32)

        acc = lax.fori_loop(0, t // tb, slot_body, jnp.zeros((4 * e, cap), F32))
        gate_ref[ex_i:ex_i + 1, :] = ((acc[ex_i:ex_i + 1, :] + acc[e + ex_i:e + ex_i + 1, :])
                                      + acc[2 * e + ex_i:2 * e + ex_i + 1, :])
        idx_ref[ex_i:ex_i + 1, :] = (acc[3 * e:3 * e + 1, :] * 128.0 + acc[3 * e + 1:3 * e + 2, :]).astype(I32)


def route(logits, cap):
    t, e = logits.shape
    return pl.pallas_call(
        functools.partial(_route_kernel, cap=cap),
        out_shape=(jax.ShapeDtypeStruct((e, cap), I32), jax.ShapeDtypeStruct((e, cap), F32)),
        scratch_shapes=[pltpu.VMEM((t, e), I32), pltpu.VMEM((t, e), I32), pltpu.VMEM((t, 4 * e), BF16)],
        compiler_params=pltpu.CompilerParams(vmem_limit_bytes=VMEM_LIMIT),
        name="route",
    )(logits)


def _issue_rows(n, fn):
    lax.fori_loop(0, n, lambda r, carry: (fn(r), carry)[1], 0, unroll=DMA_UNROLL)


def _gather_kernel(idx_sm, h_hbm, gain_ref, sh_ref, sc_ref, o_ref, buf, rinv_ref, mul_ref, sh_b_ref, sems,
                   *, sb, n_lat_slots):
    ex_i = pl.program_id(0)
    n_sub = buf.shape[0] // sb

    def row_copy(r, tok, j):
        return pltpu.make_async_copy(h_hbm.at[pl.ds(tok, 1), :], buf.at[pl.ds(r, 1), :], sems.at[j])

    def start_sub(j, carry):
        _issue_rows(sb, lambda r: row_copy(j * sb + r, idx_sm[ex_i, j * sb + r], j).start())
        return carry

    lax.fori_loop(0, n_sub, start_sub, 0)
    for kind in range(2):
        mul_ref[kind] = jnp.broadcast_to(gain_ref[...] * (1.0 + sc_ref[kind:kind + 1, :]), mul_ref.shape[1:])
        sh_b_ref[kind] = jnp.broadcast_to(sh_ref[kind:kind + 1, :], sh_b_ref.shape[1:])

    def finish_sub(j, carry):
        _issue_rows(sb, lambda r: row_copy(j * sb + r, 0, j).wait())
        xs = buf[pl.ds(pl.multiple_of(j * sb, sb), sb), :]
        rinv_ref[...] = lax.rsqrt(jnp.mean(xs * xs, axis=-1, keepdims=True) + NORM_EPS)

        def group(g, c2):
            r0 = pl.multiple_of(j * sb + g * ROW_GROUP, ROW_GROUP)
            rows = pl.ds(r0, ROW_GROUP)
            kind = (r0 >= n_lat_slots).astype(I32)
            y = buf[rows, :] * rinv_ref[pl.ds(pl.multiple_of(g * ROW_GROUP, ROW_GROUP), ROW_GROUP), :]
            o_ref[rows, :] = (y * mul_ref[kind] + sh_b_ref[kind]).astype(o_ref.dtype)
            return c2

        lax.fori_loop(0, sb // ROW_GROUP, group, 0, unroll=True)
        return carry

    lax.fori_loop(0, n_sub, finish_sub, 0)


def moe_gather(h, idx, gain, mod, k_shift, k_scale, n_lat_slots, sb):
    e, ct = idx.shape
    d = h.shape[1]
    assert n_lat_slots % ROW_GROUP == 0
    grid_spec = pltpu.PrefetchScalarGridSpec(
        num_scalar_prefetch=1,
        grid=(e,),
        in_specs=[pl.BlockSpec(memory_space=pl.ANY),
                  pl.BlockSpec((1, d), lambda i, idx_sm: (0, 0)),
                  pl.BlockSpec((8, d), lambda i, idx_sm: (0, k_shift)),
                  pl.BlockSpec((8, d), lambda i, idx_sm: (0, k_scale))],
        out_specs=pl.BlockSpec((None, ct, d), lambda i, idx_sm: (i, 0, 0)),
        scratch_shapes=[pltpu.VMEM((ct, d), F32), pltpu.VMEM((sb, 1), F32),
                        pltpu.VMEM((2, ROW_GROUP, d), F32), pltpu.VMEM((2, ROW_GROUP, d), F32),
                        pltpu.SemaphoreType.DMA((ct // sb,))],
    )
    return pl.pallas_call(
        functools.partial(_gather_kernel, sb=sb, n_lat_slots=n_lat_slots),
        grid_spec=grid_spec,
        out_shape=jax.ShapeDtypeStruct((e, ct, d), BF16),
        compiler_params=_params("arbitrary"),
        name="moe_gather",
    )(idx, h, gain.reshape(1, d), mod, mod)


def _ffn_up_kernel(x_ref, wg_ref, wu_ref, o_ref):
    x = x_ref[...]
    g = jnp.dot(x, wg_ref[...].astype(BF16), preferred_element_type=F32)
    u = jnp.dot(x, wu_ref[...].astype(BF16), preferred_element_type=F32)
    o_ref[...] = (_silu(g) * u).astype(o_ref.dtype)


def ffn_up(xg, w_gate, w_up, li):
    e, ct, d = xg.shape
    f = w_gate.shape[3]
    fb = 256
    return pl.pallas_call(
        _ffn_up_kernel,
        grid=(e, f // fb),
        in_specs=[pl.BlockSpec((None, ct, d), lambda i, j: (i, 0, 0)),
                  pl.BlockSpec((None, None, d, fb), lambda i, j: (li, i, 0, j)),
                  pl.BlockSpec((None, None, d, fb), lambda i, j: (li, i, 0, j))],
        out_specs=pl.BlockSpec((None, ct, fb), lambda i, j: (i, 0, j)),
        out_shape=jax.ShapeDtypeStruct((e, ct, f), BF16),
        compiler_params=_params("arbitrary", "arbitrary"),
        name="ffn_up",
    )(xg, w_gate, w_up)


def _ffn_down_kernel(h_ref, wd_ref, gate_ref, g2_ref, o_ref, *, n_lat_slots):
    acc = jnp.dot(h_ref[...], wd_ref[...].astype(BF16), preferred_element_type=F32)
    is_ctx = lax.broadcasted_iota(I32, (acc.shape[0], 1), 0) >= n_lat_slots
    g2 = jnp.where(is_ctx, g2_ref[1:2, :], g2_ref[0:1, :])
    o_ref[...] = (acc * gate_ref[...] * g2).astype(o_ref.dtype)


def ffn_down(hid, w_down, li, gate, mod, k_gate, n_lat_slots):
    e, ct, f = hid.shape
    d = w_down.shape[3]
    nb = _largest_divisor(d, (1024, 512, 256, 128))
    nblk = d // nb
    return pl.pallas_call(
        functools.partial(_ffn_down_kernel, n_lat_slots=n_lat_slots),
        grid=(e, nblk),
        in_specs=[pl.BlockSpec((None, ct, f), lambda i, j: (i, 0, 0)),
                  pl.BlockSpec((None, None, f, nb), lambda i, j: (li, i, 0, j)),
                  pl.BlockSpec((None, ct, 1), lambda i, j: (i, 0, 0)),
                  pl.BlockSpec((8, nb), lambda i, j: (0, k_gate * nblk + j))],
        out_specs=pl.BlockSpec((None, ct, nb), lambda i, j: (i, 0, j)),
        out_shape=jax.ShapeDtypeStruct((e, ct, d), BF16),
        compiler_params=_params("arbitrary", "arbitrary"),
        name="ffn_down",
    )(hid, w_down, gate[:, :, None], mod)


def _scatter_kernel(idx_sm, y_ref, acc_in, acc_out, buf, sems_in, sem_out, *, sb):
    del acc_in
    ex_i = pl.program_id(0)
    ct = buf.shape[0]
    n_sub = ct // sb

    def fetch(r, tok, j):
        return pltpu.make_async_copy(acc_out.at[pl.ds(tok, 1), :], buf.at[pl.ds(r, 1), :], sems_in.at[j])

    def put(r, tok):
        return pltpu.make_async_copy(buf.at[pl.ds(r, 1), :], acc_out.at[pl.ds(tok, 1), :], sem_out)

    def start_sub(j, carry):
        _issue_rows(sb, lambda r: fetch(j * sb + r, idx_sm[ex_i, j * sb + r], j).start())
        return carry

    lax.fori_loop(0, n_sub, start_sub, 0)

    def finish_sub(j, carry):
        _issue_rows(sb, lambda r: fetch(j * sb + r, 0, j).wait())
        rows = pl.ds(pl.multiple_of(j * sb, sb), sb)
        buf[rows, :] = buf[rows, :] + y_ref[rows, :].astype(F32)
        _issue_rows(sb, lambda r: put(j * sb + r, idx_sm[ex_i, j * sb + r]).start())
        return carry

    lax.fori_loop(0, n_sub, finish_sub, 0)
    _issue_rows(ct, lambda r: put(r, 0).wait())


def moe_scatter_add(acc, idx, y, sb):
    e, ct = idx.shape
    rows, d = acc.shape
    grid_spec = pltpu.PrefetchScalarGridSpec(
        num_scalar_prefetch=1,
        grid=(e,),
        in_specs=[pl.BlockSpec((None, ct, d), lambda i, idx_sm: (i, 0, 0)),
                  pl.BlockSpec(memory_space=pl.ANY)],
        out_specs=pl.BlockSpec(memory_space=pl.ANY),
        scratch_shapes=[pltpu.VMEM((ct, d), F32), pltpu.SemaphoreType.DMA((ct // sb,)),
                        pltpu.SemaphoreType.DMA(())],
    )
    return pl.pallas_call(
        functools.partial(_scatter_kernel, sb=sb),
        grid_spec=grid_spec,
        out_shape=jax.ShapeDtypeStruct((rows, d), F32),
        input_output_aliases={2: 0},
        compiler_params=_params("arbitrary"),
        name="moe_scatter_add",
    )(idx, y, acc)


def expert_choice_moe(h, gain, mod, w_router, w_gate, w_up, w_down, li, groups, t_lat):
    e = w_router.shape[2]
    logits = norm_logits(h, gain, mod, 3, 4, w_router, li, t_lat)
    idx_parts, gate_parts = [], []
    for start, count in groups:
        idx_g, gate_g = route(logits[start:start + count], EC_CAPACITY_FACTOR * count // e)
        idx_parts.append(idx_g + start)
        gate_parts.append(gate_g)
    idx = jnp.concatenate(idx_parts, axis=1)
    gate = jnp.concatenate(gate_parts, axis=1)
    ct = idx.shape[1]
    sb = _largest_divisor(ct, (128, 96, 64, 32, 16))
    n_lat_slots = EC_CAPACITY_FACTOR * t_lat // e
    xg = moe_gather(h, idx, gain, mod, 3, 4, n_lat_slots, sb)
    hid = ffn_up(xg, w_gate, w_up, li)
    y = ffn_down(hid, w_down, li, gate, mod, 5, n_lat_slots)
    return moe_scatter_add(h, idx, y, sb)


def _axial_tables(t_lat, t_ctx, rot_dim):
    axis_dim = rot_dim // 2
    tok = jnp.arange(t_lat)
    row = (tok // GRID_W).astype(F32)
    col = (tok % GRID_W).astype(F32)
    inv_freq = ROPE_BASE ** (-jnp.arange(0, axis_dim, 2, dtype=F32) / axis_dim)
    ar, ac = row[:, None] * inv_freq, col[:, None] * inv_freq
    cos = jnp.concatenate([jnp.cos(ar), jnp.cos(ar), jnp.cos(ac), jnp.cos(ac)], axis=-1)
    sin = jnp.concatenate([-jnp.sin(ar), jnp.sin(ar), -jnp.sin(ac), jnp.sin(ac)], axis=-1)
    cos = jnp.concatenate([cos, jnp.ones((t_ctx, rot_dim), F32)], axis=0)
    sin = jnp.concatenate([sin, jnp.zeros((t_ctx, rot_dim), F32)], axis=0)
    return cos, sin


def kernel(x, c, ctx, c_ctx, ada_w, ada_b, norm_mix, norm_ffn, diff_w_in, diff_w_out, diff_q_norm, diff_k_norm, diff_lambda_q1, diff_lambda_k1, diff_lambda_q2, diff_lambda_k2, diff_subln, ret_w_in, ret_w_out, ret_decay_fwd, ret_decay_bwd, router_w, expert_w_gate, expert_w_up, expert_w_down):
    batch, t_lat, d = x.shape
    t_ctx = ctx.shape[1]
    depth = ada_w.shape[0]
    assert batch == 1 and depth == 2 and t_lat % ROW_TILE == 0 and t_ctx == ROW_TILE
    t_all = t_lat + t_ctx
    diff_heads = d // DIFF_V_DIM
    ret_heads = d // RET_QK_DIM

    h = jnp.concatenate([x[0], ctx[0]], axis=0)
    cvec = jnp.zeros((8, d), F32).at[0].set(c[0]).at[1].set(c_ctx)
    cos_d, sin_d = _axial_tables(t_lat, t_ctx, DIFF_HEAD_DIM)
    cos_r, sin_r = _axial_tables(t_lat, t_ctx, RET_QK_DIM)

    mod = adaln(cvec, ada_w, ada_b, 0)
    n1 = norm_mod(h, norm_mix[0], mod, 0, 1, t_lat)
    qkv = mm_diff_in(n1, diff_w_in, 0, diff_q_norm[0], diff_k_norm[0], cos_d, sin_d)
    lam_vecs = (diff_lambda_q1[0], diff_lambda_k1[0], diff_lambda_q2[0], diff_lambda_k2[0])
    lambda_init = 0.8 - 0.6 * math.exp(-0.3 * 0)
    bq = _largest_divisor(t_lat, (512, 256))
    kc = _largest_divisor(t_all, (2816, 1408, 768, 256))
    att_lat = diff_attn(qkv, lam_vecs, diff_subln[0], diff_heads, (0, t_lat), (0, t_all), bq, kc, lambda_init)
    att_ctx = diff_attn(qkv, lam_vecs, diff_subln[0], diff_heads, (t_lat, t_ctx), (t_lat, t_ctx), t_ctx, t_ctx,
                        lambda_init)
    h = mm_resid(jnp.concatenate([att_lat, att_ctx], axis=0), diff_w_out, 0, h, mod, 2, t_lat)
    h = expert_choice_moe(h, norm_ffn[0], mod, router_w, expert_w_gate, expert_w_up, expert_w_down, 0,
                          [(0, t_lat), (t_lat, t_ctx)], t_lat)

    mod = adaln(cvec, ada_w, ada_b, 1)
    n1 = norm_mod(h, norm_mix[1], mod, 0, 1, t_lat)
    proj = mm_ret_in(n1, ret_w_in, 0, cos_r, sin_r)
    gated = retention_dir(proj, ret_decay_fwd[0], ret_heads, t_lat, backward=False)
    gated = retention_dir(proj, ret_decay_bwd[0], ret_heads, t_lat, backward=True, prev=gated)
    h = mm_resid(gated, ret_w_out, 0, h, mod, 2, t_lat, single_buffer_w=True)
    h = expert_choice_moe(h, norm_ffn[1], mod, router_w, expert_w_gate, expert_w_up, expert_w_down, 1,
                          [(0, t_lat)], t_lat)
    return h[None]
```

```python
import functools
import math

import jax
import jax.numpy as jnp
from jax import lax
from jax.experimental import pallas as pl
from jax.experimental.pallas import tpu as pltpu

F32 = jnp.float32
BF16 = jnp.bfloat16
I32 = jnp.int32

NORM_EPS = 1e-6
GRID_W = 64
ROPE_BASE = 10000.0
DIFF_HEAD_DIM = 128
DIFF_V_DIM = 2 * DIFF_HEAD_DIM
RET_QK_DIM = 256
RET_V_DIM = 2 * RET_QK_DIM
RET_CHUNK = 256
EC_CAPACITY_FACTOR = 2
ROW_TILE = 256
ROW_GROUP = 16
DMA_UNROLL = 8
MM_PART_ROWS = (256, 352)
LANES = 128
V7X_VMEM_BYTES = 64 * 1024 * 1024
VMEM_LIMIT = V7X_VMEM_BYTES - 8 * 1024 * 1024


def _params(*sem):
    return pltpu.CompilerParams(dimension_semantics=sem, vmem_limit_bytes=VMEM_LIMIT)


def _silu(x):
    half = 0.5 * x
    return half + half * jnp.tanh(half)


def _rms(x):
    return x * lax.rsqrt(jnp.mean(x * x, axis=-1, keepdims=True) + NORM_EPS)


def _largest_divisor(n, candidates):
    for cand in candidates:
        if n % cand == 0:
            return cand
    raise ValueError(f"no tile in {candidates} divides {n}")


def _mm_tiles(m, k, f32_rows=False):
    bn = 512 if k <= 4096 else 256
    if k > 4096:
        rows = (512, 256)
    elif f32_rows:
        rows = (1024, 768, 512, 256)
    else:
        rows = (1408, 1024, 768, 512, 256)
    return _largest_divisor(m, rows), bn


def _adaln_kernel(c_ref, w_ref, b_ref, o_ref):
    s = _silu(c_ref[...])
    o_ref[...] = jnp.dot(s.astype(BF16), w_ref[...].astype(BF16), preferred_element_type=F32) + b_ref[...]


def adaln(cvec, w, b, li):
    _, d, n = w.shape
    bn = 512
    return pl.pallas_call(
        _adaln_kernel,
        grid=(n // bn,),
        in_specs=[pl.BlockSpec((8, d), lambda j: (0, 0)),
                  pl.BlockSpec((None, d, bn), lambda j: (li, 0, j)),
                  pl.BlockSpec((None, 1, bn), lambda j: (li, 0, j))],
        out_specs=pl.BlockSpec((8, bn), lambda j: (0, j)),
        out_shape=jax.ShapeDtypeStruct((8, n), F32),
        compiler_params=_params("arbitrary"),
        name="adaln",
    )(cvec, w, b[:, None, :])


def _mod_rows(mod_ref, is_ctx):
    return jnp.where(is_ctx, mod_ref[1:2, :], mod_ref[0:1, :])


def _norm_mod_kernel(h_ref, gain_ref, sh_ref, sc_ref, o_ref, *, n_lat_tiles):
    is_ctx = pl.program_id(0) >= n_lat_tiles
    y = _rms(h_ref[...]) * gain_ref[...]
    o_ref[...] = (y * (1.0 + _mod_rows(sc_ref, is_ctx)) + _mod_rows(sh_ref, is_ctx)).astype(o_ref.dtype)


def norm_mod(h, gain, mod, k_shift, k_scale, t_lat):
    rows, d = h.shape
    return pl.pallas_call(
        functools.partial(_norm_mod_kernel, n_lat_tiles=t_lat // ROW_TILE),
        grid=(rows // ROW_TILE,),
        in_specs=[pl.BlockSpec((ROW_TILE, d), lambda i: (i, 0)),
                  pl.BlockSpec((1, d), lambda i: (0, 0)),
                  pl.BlockSpec((8, d), lambda i: (0, k_shift)),
                  pl.BlockSpec((8, d), lambda i: (0, k_scale))],
        out_specs=pl.BlockSpec((ROW_TILE, d), lambda i: (i, 0)),
        out_shape=jax.ShapeDtypeStruct((rows, d), BF16),
        compiler_params=_params("arbitrary"),
        name="norm_mod",
    )(h, gain.reshape(1, d), mod, mod)


def _mm_parts(a_ref, w_ref, wbf_ref):
    @pl.when(pl.program_id(1) == 0)
    def _():
        wbf_ref[...] = w_ref[...].astype(BF16)

    bm = a_ref.shape[0]
    part = _largest_divisor(bm, MM_PART_ROWS)
    for r in range(bm // part):
        rows = slice(r * part, (r + 1) * part)
        yield rows, jnp.dot(a_ref[rows, :], wbf_ref[...], preferred_element_type=F32)


def _rot_half_64(x):
    lane = lax.broadcasted_iota(I32, (1, LANES), 1)
    first = (lane % 64) < 32
    return jnp.where(first, pltpu.roll(x, 96, 1), pltpu.roll(x, 32, 1))


def _mm_diff_in_kernel(a_ref, w_ref, qg_ref, kg_ref, cos_ref, sin_ref, o_ref, wbf_ref, *, nq, nqk, qscale):
    j = pl.program_id(0)
    is_q, is_qk = j < nq, j < nqk
    gain = jnp.where(is_q, qg_ref[...], kg_ref[...])
    scale = jnp.where(is_q, qscale, 1.0)
    for rows, acc in _mm_parts(a_ref, w_ref, wbf_ref):
        cos, sin = cos_ref[rows, :], sin_ref[rows, :]
        for cidx in range(acc.shape[1] // LANES):
            sl = slice(cidx * LANES, (cidx + 1) * LANES)
            x = acc[:, sl]
            y = _rms(x) * gain
            y = (y * cos + _rot_half_64(y) * sin) * scale
            o_ref[rows, sl] = jnp.where(is_qk, y, x).astype(o_ref.dtype)


def mm_diff_in(a, w, li, q_gain, k_gain, cos, sin):
    m, k = a.shape
    n = w.shape[2]
    d = n // 3
    bm, bn = _mm_tiles(m, k)
    kern = functools.partial(_mm_diff_in_kernel, nq=d // bn, nqk=2 * d // bn,
                             qscale=DIFF_HEAD_DIM ** -0.5 * math.log2(math.e))
    return pl.pallas_call(
        kern,
        grid=(n // bn, m // bm),
        in_specs=[pl.BlockSpec((bm, k), lambda j, i: (i, 0)),
                  pl.BlockSpec((None, k, bn), lambda j, i: (li, 0, j)),
                  pl.BlockSpec((1, LANES), lambda j, i: (0, 0)),
                  pl.BlockSpec((1, LANES), lambda j, i: (0, 0)),
                  pl.BlockSpec((bm, LANES), lambda j, i: (i, 0)),
                  pl.BlockSpec((bm, LANES), lambda j, i: (i, 0))],
        out_specs=pl.BlockSpec((bm, bn), lambda j, i: (i, j)),
        out_shape=jax.ShapeDtypeStruct((m, n), BF16),
        scratch_shapes=[pltpu.VMEM((k, bn), BF16)],
        compiler_params=_params("arbitrary", "arbitrary"),
        name="mm_diff_in",
    )(a, w, q_gain.reshape(1, LANES), k_gain.reshape(1, LANES), cos, sin)


def _mm_ret_in_kernel(a_ref, w_ref, cos_ref, sin_ref, o_ref, wbf_ref, *, nq, nqk, kscale):
    j = pl.program_id(0)
    is_qk = j < nqk
    scale = jnp.where(j < nq, 1.0, kscale)
    for rows, acc in _mm_parts(a_ref, w_ref, wbf_ref):
        for cidx in range(acc.shape[1] // LANES):
            sl = slice(cidx * LANES, (cidx + 1) * LANES)
            tl = slice((cidx % 2) * LANES, (cidx % 2 + 1) * LANES)
            x = acc[:, sl]
            y = (x * cos_ref[rows, tl] + pltpu.roll(x, 64, 1) * sin_ref[rows, tl]) * scale
            o_ref[rows, sl] = jnp.where(is_qk, y, x).astype(o_ref.dtype)


def mm_ret_in(a, w, li, cos, sin):
    m, k = a.shape
    n = w.shape[2]
    d = n // 8
    bm, bn = _mm_tiles(m, k)
    kern = functools.partial(_mm_ret_in_kernel, nq=d // bn, nqk=2 * d // bn, kscale=RET_QK_DIM ** -0.5)
    return pl.pallas_call(
        kern,
        grid=(n // bn, m // bm),
        in_specs=[pl.BlockSpec((bm, k), lambda j, i: (i, 0)),
                  pl.BlockSpec((None, k, bn), lambda j, i: (li, 0, j)),
                  pl.BlockSpec((bm, RET_QK_DIM), lambda j, i: (i, 0)),
                  pl.BlockSpec((bm, RET_QK_DIM), lambda j, i: (i, 0))],
        out_specs=pl.BlockSpec((bm, bn), lambda j, i: (i, j)),
        out_shape=jax.ShapeDtypeStruct((m, n), BF16),
        scratch_shapes=[pltpu.VMEM((k, bn), BF16)],
        compiler_params=_params("arbitrary", "arbitrary"),
        name="mm_ret_in",
    )(a, w, cos, sin)


def _mm_resid_kernel(a_ref, w_ref, h_ref, g_ref, o_ref, wbf_ref, *, t_lat):
    bm = a_ref.shape[0]
    for rows, acc in _mm_parts(a_ref, w_ref, wbf_ref):
        row = pl.program_id(1) * bm + rows.start + lax.broadcasted_iota(I32, (acc.shape[0], 1), 0)
        gate = jnp.where(row >= t_lat, g_ref[1:2, :], g_ref[0:1, :])
        o_ref[rows, :] = h_ref[rows, :] + gate * acc


def mm_resid(a, w, li, h, mod, k_gate, t_lat, single_buffer_w=False):
    m, k = a.shape
    n = w.shape[2]
    bm, bn = _mm_tiles(m, k, f32_rows=True)
    if single_buffer_w:
        bn = 512
        w_spec = pl.BlockSpec((None, k, bn), lambda j, i: (li, 0, j), pipeline_mode=pl.Buffered(1))
    else:
        w_spec = pl.BlockSpec((None, k, bn), lambda j, i: (li, 0, j))
    nb = n // bn
    return pl.pallas_call(
        functools.partial(_mm_resid_kernel, t_lat=t_lat),
        grid=(nb, m // bm),
        in_specs=[pl.BlockSpec((bm, k), lambda j, i: (i, 0)),
                  w_spec,
                  pl.BlockSpec((bm, bn), lambda j, i: (i, j)),
                  pl.BlockSpec((8, bn), lambda j, i: (0, k_gate * nb + j))],
        out_specs=pl.BlockSpec((bm, bn), lambda j, i: (i, j)),
        out_shape=jax.ShapeDtypeStruct((m, n), F32),
        scratch_shapes=[pltpu.VMEM((k, bn), BF16)],
        compiler_params=_params("arbitrary", "arbitrary"),
        name="mm_resid",
    )(a, w, h, mod)


def _diff_attn_kernel(lq1_ref, lk1_ref, lq2_ref, lk2_ref, sub_ref, q_ref, k_ref, v_ref,
                      o_ref, s_a, s_b, m1, l1, a1, m2, l2, a2, *, kc, lambda_init):
    hd = DIFF_HEAD_DIM
    n_chunks = k_ref.shape[0] // kc
    maps = ((m1, l1, a1), (m2, l2, a2))
    for m_ref, l_ref, a_ref in maps:
        m_ref[...] = jnp.full(m_ref.shape, -jnp.inf, F32)
        l_ref[...] = jnp.zeros(l_ref.shape, F32)
        a_ref[...] = jnp.zeros(a_ref.shape, F32)
    dn = (((1,), (1,)), ((), ()))

    def chunk_rows(c):
        return pl.ds(pl.multiple_of(c * kc, kc), kc)

    def scores(c, s_ref, half):
        rows = chunk_rows(c)
        cols = slice(half * hd, (half + 1) * hd)
        s_ref[half] = lax.dot_general(q_ref[:, cols], k_ref[rows, cols], dn, preferred_element_type=F32)

    def update(c, s_ref, half):
        vs = v_ref[chunk_rows(c), :]
        m_ref, l_ref, a_ref = maps[half]
        s = s_ref[half]
        m_old = m_ref[...]
        m_new = jnp.maximum(m_old, jnp.max(s, axis=-1, keepdims=True))
        alpha = jnp.exp2(m_old - m_new)
        p = jnp.exp2(s - m_new)
        l_ref[...] = alpha * l_ref[...] + jnp.sum(p, axis=-1, keepdims=True)
        a_ref[...] = alpha * a_ref[...] + jnp.dot(p.astype(BF16), vs, preferred_element_type=F32)
        m_ref[...] = m_new

    def step(c_next, s_next, c_cur, s_cur):
        for half in range(2):
            if c_next is not None:
                scores(c_next, s_next, half)
            update(c_cur, s_cur, half)

    scores(0, s_a, 0)
    scores(0, s_a, 1)
    n_pairs = (n_chunks - 1) // 2

    def pair(i, carry):
        c = 2 * i
        step(c + 1, s_b, c, s_a)
        step(c + 2, s_a, c + 1, s_b)
        return carry

    lax.fori_loop(0, n_pairs, pair, 0)
    if n_chunks % 2 == 0:
        step(n_chunks - 1, s_b, n_chunks - 2, s_a)
        step(None, None, n_chunks - 1, s_b)
    else:
        step(None, None, n_chunks - 1, s_a)

    lam = (jnp.exp(jnp.sum(lq1_ref[...] * lk1_ref[...], keepdims=True))
           - jnp.exp(jnp.sum(lq2_ref[...] * lk2_ref[...], keepdims=True))) + lambda_init
    o = a1[...] / l1[...] - lam * (a2[...] / l2[...])
    o_ref[...] = (_rms(o) * sub_ref[...] * (1.0 - lambda_init)).astype(o_ref.dtype)


def diff_attn(qkv, lam_vecs, subln, n_heads, q_rows, kv_rows, bq, kc, lambda_init):
    d = n_heads * DIFF_V_DIM
    q0, nq = q_rows
    k0, nk = kv_rows
    assert q0 % bq == 0 and nq % bq == 0 and k0 % nk == 0 and nk % kc == 0
    qb0, kb0 = q0 // bq, k0 // nk
    vec = pl.BlockSpec((1, LANES), lambda h, i: (0, 0))
    in_specs = [vec, vec, vec, vec,
                pl.BlockSpec((1, DIFF_V_DIM), lambda h, i: (0, 0)),
                pl.BlockSpec((bq, DIFF_V_DIM), lambda h, i: (qb0 + i, h)),
                pl.BlockSpec((nk, DIFF_V_DIM), lambda h, i: (kb0, n_heads + h)),
                pl.BlockSpec((nk, DIFF_V_DIM), lambda h, i: (kb0, 2 * n_heads + h))]
    args = [v.reshape(1, LANES) for v in lam_vecs] + [subln.reshape(1, DIFF_V_DIM), qkv, qkv, qkv]
    kern = functools.partial(_diff_attn_kernel, kc=kc, lambda_init=lambda_init)
    return pl.pallas_call(
        kern,
        grid=(n_heads, nq // bq),
        in_specs=in_specs,
        out_specs=pl.BlockSpec((bq, DIFF_V_DIM), lambda h, i: (i, h)),
        out_shape=jax.ShapeDtypeStruct((nq, d), BF16),
        scratch_shapes=[pltpu.VMEM((2, bq, kc), F32), pltpu.VMEM((2, bq, kc), F32),
                        pltpu.VMEM((bq, 1), F32), pltpu.VMEM((bq, 1), F32), pltpu.VMEM((bq, DIFF_V_DIM), F32),
                        pltpu.VMEM((bq, 1), F32), pltpu.VMEM((bq, 1), F32), pltpu.VMEM((bq, DIFF_V_DIM), F32)],
        compiler_params=_params("arbitrary", "arbitrary"),
        name="diff_attn",
    )(*args)


def _retention_kernel(dec_ref, q_ref, k_ref, v_ref, g_ref, kc_ref, vc_ref, *rest, backward, has_prev):
    if has_prev:
        prev_ref, o_ref, s_ref, intra_ref, qd_ref, kd_ref, cd_ref = rest
    else:
        prev_ref = None
        o_ref, s_ref, intra_ref, qd_ref, kd_ref, cd_ref = rest
    c = RET_CHUNK
    n_chunks = q_ref.shape[0] // c
    dn_t = (((0,), (0,)), ((), ()))
    dn_nt = (((1,), (1,)), ((), ()))

    @pl.when(pl.program_id(1) == 0)
    def _():
        lg = jnp.log1p(-jnp.exp(dec_ref[0]))[0:1, 0:1]
        i = lax.broadcasted_iota(I32, (c, c), 0)
        j = lax.broadcasted_iota(I32, (c, c), 1)
        pos = lax.broadcasted_iota(I32, (c, 1), 0).astype(F32)
        rel = (j - i) if backward else (i - j)
        relf = jnp.maximum(rel, 0).astype(F32)
        intra_ref[...] = jnp.where(rel >= 0, jnp.exp(lg * relf), 0.0)
        if backward:
            qd_ref[...] = jnp.exp(lg * (c - pos))
            kd_ref[...] = jnp.exp(lg * pos)
        else:
            qd_ref[...] = jnp.exp(lg * (pos + 1.0))
            kd_ref[...] = jnp.exp(lg * (c - 1.0 - pos))
        cd_ref[...] = jnp.broadcast_to(jnp.exp(lg * float(c)), cd_ref.shape)
        kd = (kc_ref[...].astype(F32) * kd_ref[...]).astype(BF16)
        s_ref[...] = lax.dot_general(kd, vc_ref[...], dn_t, preferred_element_type=F32)

    def body(t, carry):
        cc = (n_chunks - 1 - t) if backward else t
        rows = pl.ds(pl.multiple_of(cc * c, c), c)
        q, k, v = q_ref[rows, :], k_ref[rows, :], v_ref[rows, :]
        scores = lax.dot_general(q, k, dn_nt, preferred_element_type=F32) * intra_ref[...]
        s_old = s_ref[...]
        qd = (q.astype(F32) * qd_ref[...]).astype(BF16)
        o = (jnp.dot(scores.astype(BF16), v, preferred_element_type=F32)
             + jnp.dot(qd, s_old.astype(BF16), preferred_element_type=F32))
        kd = (k.astype(F32) * kd_ref[...]).astype(BF16)
        s_ref[...] = s_old * cd_ref[0:1, 0:1] + lax.dot_general(kd, v, dn_t, preferred_element_type=F32)
        out = _silu(g_ref[rows, :].astype(F32)) * _rms(o)
        if has_prev:
            out = out + prev_ref[rows, :].astype(F32)
        o_ref[rows, :] = out.astype(o_ref.dtype)
        return carry

    lax.fori_loop(0, n_chunks, body, 0, unroll=2)


def retention_dir(proj, decay, n_heads, t_lat, backward, prev=None):
    c = RET_CHUNK
    d = n_heads * RET_QK_DIM
    t_ctx = proj.shape[0] - t_lat
    assert t_ctx == c and t_lat % c == 0
    rows = _largest_divisor(t_lat, (2048, 1024, 512, 256))
    n_steps = t_lat // rows
    step = (lambda s: n_steps - 1 - s) if backward else (lambda s: s)
    qb, vb = d // RET_QK_DIM, 2 * d // RET_V_DIM
    gb = (6 if backward else 4) * d // RET_V_DIM
    ctx_blk = t_lat // c
    in_specs = [pl.BlockSpec((1, 8, LANES), lambda h, s: (h, 0, 0)),
                pl.BlockSpec((rows, RET_QK_DIM), lambda h, s: (step(s), h)),
                pl.BlockSpec((rows, RET_QK_DIM), lambda h, s: (step(s), qb + h)),
                pl.BlockSpec((rows, RET_V_DIM), lambda h, s: (step(s), vb + h)),
                pl.BlockSpec((rows, RET_V_DIM), lambda h, s: (step(s), gb + h)),
                pl.BlockSpec((c, RET_QK_DIM), lambda h, s: (ctx_blk, qb + h)),
                pl.BlockSpec((c, RET_V_DIM), lambda h, s: (ctx_blk, vb + h))]
    args = [jnp.broadcast_to(decay.astype(F32)[:, None, None], (n_heads, 8, LANES)), proj, proj, proj, proj, proj, proj]
    if prev is not None:
        in_specs.append(pl.BlockSpec((rows, RET_V_DIM), lambda h, s: (step(s), h)))
        args.append(prev)
    kern = functools.partial(_retention_kernel, backward=backward, has_prev=prev is not None)
    return pl.pallas_call(
        kern,
        grid=(n_heads, n_steps),
        in_specs=in_specs,
        out_specs=pl.BlockSpec((rows, RET_V_DIM), lambda h, s: (step(s), h)),
        out_shape=jax.ShapeDtypeStruct((t_lat, n_heads * RET_V_DIM), BF16),
        scratch_shapes=[pltpu.VMEM((RET_QK_DIM, RET_V_DIM), F32), pltpu.VMEM((c, c), F32),
                        pltpu.VMEM((c, 1), F32), pltpu.VMEM((c, 1), F32), pltpu.VMEM((8, LANES), F32)],
        compiler_params=_params("arbitrary", "arbitrary"),
        name="retention_bwd" if backward else "retention_fwd",
    )(*args)


def _norm_logits_kernel(h_ref, gain_ref, sh_ref, sc_ref, wr_ref, lg_ref, *, n_lat_tiles):
    is_ctx = pl.program_id(0) >= n_lat_tiles
    y = _rms(h_ref[...]) * gain_ref[...]
    n2 = (y * (1.0 + _mod_rows(sc_ref, is_ctx)) + _mod_rows(sh_ref, is_ctx)).astype(BF16)
    lg_ref[...] = jnp.dot(n2, wr_ref[...].astype(BF16), preferred_element_type=F32)


def norm_logits(h, gain, mod, k_shift, k_scale, w_router, li, t_lat):
    rows, d = h.shape
    e = w_router.shape[2]
    return pl.pallas_call(
        functools.partial(_norm_logits_kernel, n_lat_tiles=t_lat // ROW_TILE),
        grid=(rows // ROW_TILE,),
        in_specs=[pl.BlockSpec((ROW_TILE, d), lambda i: (i, 0)),
                  pl.BlockSpec((1, d), lambda i: (0, 0)),
                  pl.BlockSpec((8, d), lambda i: (0, k_shift)),
                  pl.BlockSpec((8, d), lambda i: (0, k_scale)),
                  pl.BlockSpec((None, d, e), lambda i: (li, 0, 0))],
        out_specs=pl.BlockSpec((ROW_TILE, e), lambda i: (i, 0)),
        out_shape=jax.ShapeDtypeStruct((rows, e), F32),
        compiler_params=_params("arbitrary"),
        name="norm_logits",
    )(h, gain.reshape(1, d), mod, mod, w_router)


def _route_kernel(lg_ref, idx_ref, gate_ref, bits_ref, key_ref, w_ref, *, cap):
    t, e = lg_ref.shape
    pb = min(t, 256)
    tb = min(t, 1024)
    lg = lg_ref[...]
    ex = jnp.exp(lg - jnp.max(lg, axis=-1, keepdims=True))
    aff = ex / jnp.sum(ex, axis=-1, keepdims=True)
    bits_ref[...] = lax.bitcast_convert_type(aff, I32)
    ahi = aff.astype(BF16)
    r1 = aff - ahi.astype(F32)
    amid = r1.astype(BF16)
    alo = (r1 - amid.astype(F32)).astype(BF16)
    tok = lax.broadcasted_iota(I32, (t, e), 0)
    lane = lax.broadcasted_iota(I32, (t, e), 1)
    tdig = jnp.where(lane == 0, tok >> 7, jnp.where(lane == 1, tok & 127, 0)).astype(F32).astype(BF16)
    pr = lax.broadcasted_iota(I32, (e, 4 * e), 0)
    pc = lax.broadcasted_iota(I32, (e, 4 * e), 1)
    packed = jnp.zeros((t, 4 * e), F32)
    for piece, part in enumerate((ahi, amid, alo, tdig)):
        place = jnp.where(pc == pr + piece * e, 1.0, 0.0).astype(BF16)
        packed = packed + jnp.dot(part, place, preferred_element_type=F32)
    w_ref[...] = packed.astype(BF16)

    def thr_body(it, thr):
        cand = thr | (jnp.int32(1) << (30 - it))
        cnt = jnp.sum(jnp.where(bits_ref[...] >= cand, 1.0, 0.0), axis=0, keepdims=True)
        return jnp.where(cnt >= cap, cand, thr)

    thr = lax.fori_loop(0, 31, thr_body, jnp.zeros((1, e), I32))
    n_gt = jnp.sum(jnp.where(bits_ref[...] > thr, 1.0, 0.0), axis=0, keepdims=True)
    need = cap - n_gt

    ri = lax.broadcasted_iota(I32, (pb, pb), 0)
    ci = lax.broadcasted_iota(I32, (pb, pb), 1)
    ltri = jnp.where(ci < ri, 1.0, 0.0).astype(BF16)

    def pre_body(b, carry):
        ceq, csel = carry
        rows = pl.ds(pl.multiple_of(b * pb, pb), pb)
        bt = bits_ref[rows, :]
        gtb = bt > thr
        eqf = jnp.where(bt == thr, 1.0, 0.0)
        eq_rank = jnp.dot(ltri, eqf.astype(BF16), preferred_element_type=F32) + ceq
        self_ = jnp.where(gtb, 1.0, jnp.where(eq_rank < need, eqf, 0.0))
        pos = jnp.dot(ltri, self_.astype(BF16), preferred_element_type=F32) + csel
        key_ref[rows, :] = jnp.where(self_ > 0.0, pos, -1.0).astype(I32)
        return (ceq + jnp.sum(eqf, axis=0, keepdims=True), csel + jnp.sum(self_, axis=0, keepdims=True))

    zero = jnp.zeros((1, e), F32)
    lax.fori_loop(0, t // pb, pre_body, (zero, zero))

    slot = lax.broadcasted_iota(I32, (1, cap), 1)
    dn_t = (((0,), (0,)), ((), ()))
    for ex_i in range(e):
        def slot_body(b, carry, ex_i=ex_i):
            rows = pl.ds(pl.multiple_of(b * tb, tb), tb)
            onehot = jnp.where(key_ref[rows, ex_i:ex_i + 1] == slot, 1.0, 0.0).astype(BF16)
            return carry + lax.dot_general(w_ref[rows, :], onehot, dn_t, preferred_element_type=F32)

        acc = lax.fori_loop(0, t // tb, slot_body, jnp.zeros((4 * e, cap), F32))
        gate_ref[ex_i:ex_i + 1, :] = ((acc[ex_i:ex_i + 1, :] + acc[e + ex_i:e + ex_i + 1, :])
                                      + acc[2 * e + ex_i:2 * e + ex_i + 1, :])
        idx_ref[ex_i:ex_i + 1, :] = (acc[3 * e:3 * e + 1, :] * 128.0 + acc[3 * e + 1:3 * e + 2, :]).astype(I32)


def route(logits, cap):
    t, e = logits.shape
    return pl.pallas_call(
        functools.partial(_route_kernel, cap=cap),
        out_shape=(jax.ShapeDtypeStruct((e, cap), I32), jax.ShapeDtypeStruct((e, cap), F32)),
        scratch_shapes=[pltpu.VMEM((t, e), I32), pltpu.VMEM((t, e), I32), pltpu.VMEM((t, 4 * e), BF16)],
        compiler_params=pltpu.CompilerParams(vmem_limit_bytes=VMEM_LIMIT),
        name="route",
    )(logits)


def _issue_rows(n, fn):
    lax.fori_loop(0, n, lambda r, carry: (fn(r), carry)[1], 0, unroll=DMA_UNROLL)


def _gather_kernel(idx_sm, h_hbm, gain_ref, sh_ref, sc_ref, o_ref, buf, rinv_ref, mul_ref, sh_b_ref, sems,
                   *, sb, n_lat_slots):
    base = pl.program_id(0) * buf.shape[0]
    n_sub = buf.shape[0] // sb

    def row_copy(r, tok, j):
        return pltpu.make_async_copy(h_hbm.at[pl.ds(tok, 1), :], buf.at[pl.ds(r, 1), :], sems.at[j])

    def start_sub(j, carry):
        _issue_rows(sb, lambda r: row_copy(j * sb + r, idx_sm[base + j * sb + r], j).start())
        return carry

    lax.fori_loop(0, n_sub, start_sub, 0)
    for kind in range(2):
        mul_ref[kind] = jnp.broadcast_to(gain_ref[...] * (1.0 + sc_ref[kind:kind + 1, :]), mul_ref.shape[1:])
        sh_b_ref[kind] = jnp.broadcast_to(sh_ref[kind:kind + 1, :], sh_b_ref.shape[1:])

    def finish_sub(j, carry):
        _issue_rows(sb, lambda r: row_copy(j * sb + r, 0, j).wait())
        xs = buf[pl.ds(pl.multiple_of(j * sb, sb), sb), :]
        rinv_ref[...] = lax.rsqrt(jnp.mean(xs * xs, axis=-1, keepdims=True) + NORM_EPS)

        def group(g, c2):
            r0 = pl.multiple_of(j * sb + g * ROW_GROUP, ROW_GROUP)
            rows = pl.ds(r0, ROW_GROUP)
            kind = (r0 >= n_lat_slots).astype(I32)
            y = buf[rows, :] * rinv_ref[pl.ds(pl.multiple_of(g * ROW_GROUP, ROW_GROUP), ROW_GROUP), :]
            o_ref[rows, :] = (y * mul_ref[kind] + sh_b_ref[kind]).astype(o_ref.dtype)
            return c2

        lax.fori_loop(0, sb // ROW_GROUP, group, 0, unroll=True)
        return carry

    lax.fori_loop(0, n_sub, finish_sub, 0)


def moe_gather(h, idx, gain, mod, k_shift, k_scale, n_lat_slots, sb):
    e, ct = idx.shape
    d = h.shape[1]
    assert n_lat_slots % ROW_GROUP == 0
    grid_spec = pltpu.PrefetchScalarGridSpec(
        num_scalar_prefetch=1,
        grid=(e,),
        in_specs=[pl.BlockSpec(memory_space=pl.ANY),
                  pl.BlockSpec((1, d), lambda i, idx_sm: (0, 0)),
                  pl.BlockSpec((8, d), lambda i, idx_sm: (0, k_shift)),
                  pl.BlockSpec((8, d), lambda i, idx_sm: (0, k_scale))],
        out_specs=pl.BlockSpec((None, ct, d), lambda i, idx_sm: (i, 0, 0)),
        scratch_shapes=[pltpu.VMEM((ct, d), F32), pltpu.VMEM((sb, 1), F32),
                        pltpu.VMEM((2, ROW_GROUP, d), F32), pltpu.VMEM((2, ROW_GROUP, d), F32),
                        pltpu.SemaphoreType.DMA((ct // sb,))],
    )
    return pl.pallas_call(
        functools.partial(_gather_kernel, sb=sb, n_lat_slots=n_lat_slots),
        grid_spec=grid_spec,
        out_shape=jax.ShapeDtypeStruct((e, ct, d), BF16),
        compiler_params=_params("arbitrary"),
        name="moe_gather",
    )(idx.reshape(-1), h, gain.reshape(1, d), mod, mod)


def _ffn_up_kernel(x_ref, wg_ref, wu_ref, o_ref):
    x = x_ref[...]
    g = jnp.dot(x, wg_ref[...].astype(BF16), preferred_element_type=F32)
    u = jnp.dot(x, wu_ref[...].astype(BF16), preferred_element_type=F32)
    o_ref[...] = (_silu(g) * u).astype(o_ref.dtype)


def ffn_up(xg, w_gate, w_up, li):
    e, ct, d = xg.shape
    f = w_gate.shape[3]
    fb = 256
    return pl.pallas_call(
        _ffn_up_kernel,
        grid=(e, f // fb),
        in_specs=[pl.BlockSpec((None, ct, d), lambda i, j: (i, 0, 0)),
                  pl.BlockSpec((None, None, d, fb), lambda i, j: (li, i, 0, j)),
                  pl.BlockSpec((None, None, d, fb), lambda i, j: (li, i, 0, j))],
        out_specs=pl.BlockSpec((None, ct, fb), lambda i, j: (i, 0, j)),
        out_shape=jax.ShapeDtypeStruct((e, ct, f), BF16),
        compiler_params=_params("arbitrary", "arbitrary"),
        name="ffn_up",
    )(xg, w_gate, w_up)


def _ffn_down_kernel(h_ref, wd_ref, gate_ref, g2_ref, o_ref, *, n_lat_slots):
    acc = jnp.dot(h_ref[...], wd_ref[...].astype(BF16), preferred_element_type=F32)
    is_ctx = lax.broadcasted_iota(I32, (acc.shape[0], 1), 0) >= n_lat_slots
    g2 = jnp.where(is_ctx, g2_ref[1:2, :], g2_ref[0:1, :])
    o_ref[...] = (acc * gate_ref[...] * g2).astype(o_ref.dtype)


def ffn_down(hid, w_down, li, gate, mod, k_gate, n_lat_slots):
    e, ct, f = hid.shape
    d = w_down.shape[3]
    nb = _largest_divisor(d, (1024, 512, 256, 128))
    nblk = d // nb
    return pl.pallas_call(
        functools.partial(_ffn_down_kernel, n_lat_slots=n_lat_slots),
        grid=(e, nblk),
        in_specs=[pl.BlockSpec((None, ct, f), lambda i, j: (i, 0, 0)),
                  pl.BlockSpec((None, None, f, nb), lambda i, j: (li, i, 0, j)),
                  pl.BlockSpec((None, ct, 1), lambda i, j: (i, 0, 0)),
                  pl.BlockSpec((8, nb), lambda i, j: (0, k_gate * nblk + j))],
        out_specs=pl.BlockSpec((None, ct, nb), lambda i, j: (i, 0, j)),
        out_shape=jax.ShapeDtypeStruct((e, ct, d), BF16),
        compiler_params=_params("arbitrary", "arbitrary"),
        name="ffn_down",
    )(hid, w_down, gate[:, :, None], mod)


def _scatter_kernel(idx_sm, y_ref, acc_in, acc_out, buf, sems_in, sem_out, *, sb):
    del acc_in
    ct = buf.shape[0]
    base = pl.program_id(0) * ct
    n_sub = ct // sb

    def fetch(r, tok, j):
        return pltpu.make_async_copy(acc_out.at[pl.ds(tok, 1), :], buf.at[pl.ds(r, 1), :], sems_in.at[j])

    def put(r, tok):
        return pltpu.make_async_copy(buf.at[pl.ds(r, 1), :], acc_out.at[pl.ds(tok, 1), :], sem_out)

    def start_sub(j, carry):
        _issue_rows(sb, lambda r: fetch(j * sb + r, idx_sm[base + j * sb + r], j).start())
        return carry

    lax.fori_loop(0, n_sub, start_sub, 0)

    def finish_sub(j, carry):
        _issue_rows(sb, lambda r: fetch(j * sb + r, 0, j).wait())
        rows = pl.ds(pl.multiple_of(j * sb, sb), sb)
        buf[rows, :] = buf[rows, :] + y_ref[rows, :].astype(F32)
        _issue_rows(sb, lambda r: put(j * sb + r, idx_sm[base + j * sb + r]).start())
        return carry

    lax.fori_loop(0, n_sub, finish_sub, 0)
    _issue_rows(ct, lambda r: put(r, 0).wait())


def moe_scatter_add(acc, idx, y, sb):
    e, ct = idx.shape
    rows, d = acc.shape
    grid_spec = pltpu.PrefetchScalarGridSpec(
        num_scalar_prefetch=1,
        grid=(e,),
        in_specs=[pl.BlockSpec((None, ct, d), lambda i, idx_sm: (i, 0, 0)),
                  pl.BlockSpec(memory_space=pl.ANY)],
        out_specs=pl.BlockSpec(memory_space=pl.ANY),
        scratch_shapes=[pltpu.VMEM((ct, d), F32), pltpu.SemaphoreType.DMA((ct // sb,)),
                        pltpu.SemaphoreType.DMA(())],
    )
    return pl.pallas_call(
        functools.partial(_scatter_kernel, sb=sb),
        grid_spec=grid_spec,
        out_shape=jax.ShapeDtypeStruct((rows, d), F32),
        input_output_aliases={2: 0},
        compiler_params=_params("arbitrary"),
        name="moe_scatter_add",
    )(idx.reshape(-1), y, acc)


def expert_choice_moe(h, gain, mod, w_router, w_gate, w_up, w_down, li, groups, t_lat):
    e = w_router.shape[2]
    logits = norm_logits(h, gain, mod, 3, 4, w_router, li, t_lat)
    idx_parts, gate_parts = [], []
    for start, count in groups:
        idx_g, gate_g = route(logits[start:start + count], EC_CAPACITY_FACTOR * count // e)
        idx_parts.append(idx_g + start)
        gate_parts.append(gate_g)
    idx = jnp.concatenate(idx_parts, axis=1)
    gate = jnp.concatenate(gate_parts, axis=1)
    ct = idx.shape[1]
    sb = _largest_divisor(ct, (128, 96, 64, 32, 16))
    n_lat_slots = EC_CAPACITY_FACTOR * t_lat // e
    xg = moe_gather(h, idx, gain, mod, 3, 4, n_lat_slots, sb)
    hid = ffn_up(xg, w_gate, w_up, li)
    y = ffn_down(hid, w_down, li, gate, mod, 5, n_lat_slots)
    return moe_scatter_add(h, idx, y, sb)


def _axial_tables(t_lat, t_ctx, rot_dim):
    axis_dim = rot_dim // 2
    tok = jnp.arange(t_lat)
    row = (tok // GRID_W).astype(F32)
    col = (tok % GRID_W).astype(F32)
    inv_freq = ROPE_BASE ** (-jnp.arange(0, axis_dim, 2, dtype=F32) / axis_dim)
    ar, ac = row[:, None] * inv_freq, col[:, None] * inv_freq
    cos = jnp.concatenate([jnp.cos(ar), jnp.cos(ar), jnp.cos(ac), jnp.cos(ac)], axis=-1)
    sin = jnp.concatenate([-jnp.sin(ar), jnp.sin(ar), -jnp.sin(ac), jnp.sin(ac)], axis=-1)
    cos = jnp.concatenate([cos, jnp.ones((t_ctx, rot_dim), F32)], axis=0)
    sin = jnp.concatenate([sin, jnp.zeros((t_ctx, rot_dim), F32)], axis=0)
    return cos, sin


def kernel(x, c, ctx, c_ctx, ada_w, ada_b, norm_mix, norm_ffn, diff_w_in, diff_w_out, diff_q_norm, diff_k_norm, diff_lambda_q1, diff_lambda_k1, diff_lambda_q2, diff_lambda_k2, diff_subln, ret_w_in, ret_w_out, ret_decay_fwd, ret_decay_bwd, router_w, expert_w_gate, expert_w_up, expert_w_down):
    batch, t_lat, d = x.shape
    t_ctx = ctx.shape[1]
    depth = ada_w.shape[0]
    assert batch == 1 and depth == 2 and t_lat % ROW_TILE == 0 and t_ctx == ROW_TILE
    t_all = t_lat + t_ctx
    diff_heads = d // DIFF_V_DIM
    ret_heads = d // RET_QK_DIM

    h = jnp.concatenate([x[0], ctx[0]], axis=0)
    cvec = jnp.zeros((8, d), F32).at[0].set(c[0]).at[1].set(c_ctx)
    cos_d, sin_d = _axial_tables(t_lat, t_ctx, DIFF_HEAD_DIM)
    cos_r, sin_r = _axial_tables(t_lat, t_ctx, RET_QK_DIM)

    mod = adaln(cvec, ada_w, ada_b, 0)
    n1 = norm_mod(h, norm_mix[0], mod, 0, 1, t_lat)
    qkv = mm_diff_in(n1, diff_w_in, 0, diff_q_norm[0], diff_k_norm[0], cos_d, sin_d)
    lam_vecs = (diff_lambda_q1[0], diff_lambda_k1[0], diff_lambda_q2[0], diff_lambda_k2[0])
    lambda_init = 0.8 - 0.6 * math.exp(-0.3 * 0)
    bq = _largest_divisor(t_lat, (512, 256))
    kc = _largest_divisor(t_all, (2816, 1408, 768, 256))
    att_lat = diff_attn(qkv, lam_vecs, diff_subln[0], diff_heads, (0, t_lat), (0, t_all), bq, kc, lambda_init)
    att_ctx = diff_attn(qkv, lam_vecs, diff_subln[0], diff_heads, (t_lat, t_ctx), (t_lat, t_ctx), t_ctx, t_ctx,
                        lambda_init)
    h = mm_resid(jnp.concatenate([att_lat, att_ctx], axis=0), diff_w_out, 0, h, mod, 2, t_lat)
    h = expert_choice_moe(h, norm_ffn[0], mod, router_w, expert_w_gate, expert_w_up, expert_w_down, 0,
                          [(0, t_lat), (t_lat, t_ctx)], t_lat)

    mod = adaln(cvec, ada_w, ada_b, 1)
    n1 = norm_mod(h, norm_mix[1], mod, 0, 1, t_lat)
    proj = mm_ret_in(n1, ret_w_in, 0, cos_r, sin_r)
    gated = retention_dir(proj, ret_decay_fwd[0], ret_heads, t_lat, backward=False)
    gated = retention_dir(proj, ret_decay_bwd[0], ret_heads, t_lat, backward=True, prev=gated)
    h = mm_resid(gated, ret_w_out, 0, h, mod, 2, t_lat, single_buffer_w=True)
    h = expert_choice_moe(h, norm_ffn[1], mod, router_w, expert_w_gate, expert_w_up, expert_w_down, 1,
                          [(0, t_lat)], t_lat)
    return h[None]
```

```python
import functools
import math

import jax
import jax.numpy as jnp
from jax import lax
from jax.experimental import pallas as pl
from jax.experimental.pallas import tpu as pltpu

F32 = jnp.float32
BF16 = jnp.bfloat16
I32 = jnp.int32

NORM_EPS = 1e-6
GRID_W = 64
ROPE_BASE = 10000.0
DIFF_HEAD_DIM = 128
DIFF_V_DIM = 2 * DIFF_HEAD_DIM
RET_QK_DIM = 256
RET_V_DIM = 2 * RET_QK_DIM
RET_CHUNK = 256
EC_CAPACITY_FACTOR = 2
ROW_TILE = 256
ROW_GROUP = 16
DMA_UNROLL = 8
MM_PART_ROWS = (256, 352)
LANES = 128
V7X_VMEM_BYTES = 64 * 1024 * 1024
VMEM_LIMIT = V7X_VMEM_BYTES - 3 * 1024 * 1024


def _params(*sem):
    return pltpu.CompilerParams(dimension_semantics=sem, vmem_limit_bytes=VMEM_LIMIT)


def _silu(x):
    half = 0.5 * x
    return half + half * jnp.tanh(half)


def _rms(x):
    return x * lax.rsqrt(jnp.mean(x * x, axis=-1, keepdims=True) + NORM_EPS)


def _largest_divisor(n, candidates):
    for cand in candidates:
        if n % cand == 0:
            return cand
    raise ValueError(f"no tile in {candidates} divides {n}")


def _mm_tiles(m, k, n_unit, f32_rows=False):
    if k > 4096:
        return _largest_divisor(m, (512, 256)), 256
    if f32_rows:
        return _largest_divisor(m, (1408, 1024, 768, 512, 256)), 512
    return _largest_divisor(m, (1024, 768, 512, 256)), _largest_divisor(n_unit, (1024, 512))


def _adaln_kernel(c_ref, w_ref, b_ref, o_ref):
    s = _silu(c_ref[...])
    o_ref[...] = jnp.dot(s.astype(BF16), w_ref[...].astype(BF16), preferred_element_type=F32) + b_ref[...]


def adaln(cvec, w, b, li):
    _, d, n = w.shape
    bn = 512
    return pl.pallas_call(
        _adaln_kernel,
        grid=(n // bn,),
        in_specs=[pl.BlockSpec((8, d), lambda j: (0, 0)),
                  pl.BlockSpec((None, d, bn), lambda j: (li, 0, j)),
                  pl.BlockSpec((None, 1, bn), lambda j: (li, 0, j))],
        out_specs=pl.BlockSpec((8, bn), lambda j: (0, j)),
        out_shape=jax.ShapeDtypeStruct((8, n), F32),
        compiler_params=_params("arbitrary"),
        name="adaln",
    )(cvec, w, b[:, None, :])


def _mod_rows(mod_ref, is_ctx):
    return jnp.where(is_ctx, mod_ref[1:2, :], mod_ref[0:1, :])


def _norm_mod_kernel(h_ref, gain_ref, sh_ref, sc_ref, o_ref, *, n_lat_tiles):
    is_ctx = pl.program_id(0) >= n_lat_tiles
    y = _rms(h_ref[...]) * gain_ref[...]
    o_ref[...] = (y * (1.0 + _mod_rows(sc_ref, is_ctx)) + _mod_rows(sh_ref, is_ctx)).astype(o_ref.dtype)


def norm_mod(h, gain, mod, k_shift, k_scale, t_lat):
    rows, d = h.shape
    return pl.pallas_call(
        functools.partial(_norm_mod_kernel, n_lat_tiles=t_lat // ROW_TILE),
        grid=(rows // ROW_TILE,),
        in_specs=[pl.BlockSpec((ROW_TILE, d), lambda i: (i, 0)),
                  pl.BlockSpec((1, d), lambda i: (0, 0)),
                  pl.BlockSpec((8, d), lambda i: (0, k_shift)),
                  pl.BlockSpec((8, d), lambda i: (0, k_scale))],
        out_specs=pl.BlockSpec((ROW_TILE, d), lambda i: (i, 0)),
        out_shape=jax.ShapeDtypeStruct((rows, d), BF16),
        compiler_params=_params("arbitrary"),
        name="norm_mod",
    )(h, gain.reshape(1, d), mod, mod)


def _mm_parts(a_ref, w_ref, wbf_ref):
    @pl.when(pl.program_id(1) == 0)
    def _():
        wbf_ref[...] = w_ref[...].astype(BF16)

    bm = a_ref.shape[0]
    part = _largest_divisor(bm, MM_PART_ROWS)
    for r in range(bm // part):
        rows = slice(r * part, (r + 1) * part)
        yield rows, jnp.dot(a_ref[rows, :], wbf_ref[...], preferred_element_type=F32)


def _rot_half_64(x):
    lane = lax.broadcasted_iota(I32, (1, LANES), 1)
    first = (lane % 64) < 32
    return jnp.where(first, pltpu.roll(x, 96, 1), pltpu.roll(x, 32, 1))


def _mm_diff_in_kernel(a_ref, w_ref, qg_ref, kg_ref, cos_ref, sin_ref, o_ref, wbf_ref, *, nq, nqk, qscale):
    j = pl.program_id(0)
    is_q, is_qk = j < nq, j < nqk
    gain = jnp.where(is_q, qg_ref[...], kg_ref[...])
    scale = jnp.where(is_q, qscale, 1.0)
    for rows, acc in _mm_parts(a_ref, w_ref, wbf_ref):
        cos, sin = cos_ref[rows, :], sin_ref[rows, :]
        for cidx in range(acc.shape[1] // LANES):
            sl = slice(cidx * LANES, (cidx + 1) * LANES)
            x = acc[:, sl]
            y = _rms(x) * gain
            y = (y * cos + _rot_half_64(y) * sin) * scale
            o_ref[rows, sl] = jnp.where(is_qk, y, x).astype(o_ref.dtype)


def mm_diff_in(a, w, li, q_gain, k_gain, cos, sin):
    m, k = a.shape
    n = w.shape[2]
    d = n // 3
    bm, bn = _mm_tiles(m, k, d)
    nqk = 2 * d // bn
    kern = functools.partial(_mm_diff_in_kernel, nq=d // bn, nqk=nqk,
                             qscale=DIFF_HEAD_DIM ** -0.5 * math.log2(math.e))
    return pl.pallas_call(
        kern,
        grid=(n // bn, m // bm),
        in_specs=[pl.BlockSpec((bm, k), lambda j, i: (i, 0)),
                  pl.BlockSpec((None, k, bn), lambda j, i: (li, 0, j)),
                  pl.BlockSpec((1, LANES), lambda j, i: (0, 0)),
                  pl.BlockSpec((1, LANES), lambda j, i: (0, 0)),
                  pl.BlockSpec((bm, LANES), lambda j, i: (jnp.where(j < nqk, i, 0), 0)),
                  pl.BlockSpec((bm, LANES), lambda j, i: (jnp.where(j < nqk, i, 0), 0))],
        out_specs=pl.BlockSpec((bm, bn), lambda j, i: (i, j)),
        out_shape=jax.ShapeDtypeStruct((m, n), BF16),
        scratch_shapes=[pltpu.VMEM((k, bn), BF16)],
        compiler_params=_params("arbitrary", "arbitrary"),
        name="mm_diff_in",
    )(a, w, q_gain.reshape(1, LANES), k_gain.reshape(1, LANES), cos, sin)


def _mm_ret_in_kernel(a_ref, w_ref, cos_ref, sin_ref, o_ref, wbf_ref, *, nq, nqk, kscale):
    j = pl.program_id(0)
    is_qk = j < nqk
    scale = jnp.where(j < nq, 1.0, kscale)
    for rows, acc in _mm_parts(a_ref, w_ref, wbf_ref):
        for cidx in range(acc.shape[1] // LANES):
            sl = slice(cidx * LANES, (cidx + 1) * LANES)
            tl = slice((cidx % 2) * LANES, (cidx % 2 + 1) * LANES)
            x = acc[:, sl]
            y = (x * cos_ref[rows, tl] + pltpu.roll(x, 64, 1) * sin_ref[rows, tl]) * scale
            o_ref[rows, sl] = jnp.where(is_qk, y, x).astype(o_ref.dtype)


def mm_ret_in(a, w, li, cos, sin):
    m, k = a.shape
    n = w.shape[2]
    d = n // 8
    bm, bn = _mm_tiles(m, k, d)
    nqk = 2 * d // bn
    kern = functools.partial(_mm_ret_in_kernel, nq=d // bn, nqk=nqk, kscale=RET_QK_DIM ** -0.5)
    return pl.pallas_call(
        kern,
        grid=(n // bn, m // bm),
        in_specs=[pl.BlockSpec((bm, k), lambda j, i: (i, 0)),
                  pl.BlockSpec((None, k, bn), lambda j, i: (li, 0, j)),
                  pl.BlockSpec((bm, RET_QK_DIM), lambda j, i: (jnp.where(j < nqk, i, 0), 0)),
                  pl.BlockSpec((bm, RET_QK_DIM), lambda j, i: (jnp.where(j < nqk, i, 0), 0))],
        out_specs=pl.BlockSpec((bm, bn), lambda j, i: (i, j)),
        out_shape=jax.ShapeDtypeStruct((m, n), BF16),
        scratch_shapes=[pltpu.VMEM((k, bn), BF16)],
        compiler_params=_params("arbitrary", "arbitrary"),
        name="mm_ret_in",
    )(a, w, cos, sin)


def _mm_resid_kernel(a_ref, w_ref, h_ref, g_ref, o_ref, wbf_ref, *, t_lat):
    bm = a_ref.shape[0]
    for rows, acc in _mm_parts(a_ref, w_ref, wbf_ref):
        row = pl.program_id(1) * bm + rows.start + lax.broadcasted_iota(I32, (acc.shape[0], 1), 0)
        gate = jnp.where(row >= t_lat, g_ref[1:2, :], g_ref[0:1, :])
        o_ref[rows, :] = h_ref[rows, :] + gate * acc


def mm_resid(a, w, li, h, mod, k_gate, t_lat, single_buffer_w=False):
    m, k = a.shape
    n = w.shape[2]
    bm, bn = _mm_tiles(m, k, n, f32_rows=True)
    if single_buffer_w:
        bn = 512
        w_spec = pl.BlockSpec((None, k, bn), lambda j, i: (li, 0, j), pipeline_mode=pl.Buffered(1))
    else:
        w_spec = pl.BlockSpec((None, k, bn), lambda j, i: (li, 0, j))
    nb = n // bn
    return pl.pallas_call(
        functools.partial(_mm_resid_kernel, t_lat=t_lat),
        grid=(nb, m // bm),
        in_specs=[pl.BlockSpec((bm, k), lambda j, i: (i, 0)),
                  w_spec,
                  pl.BlockSpec((bm, bn), lambda j, i: (i, j)),
                  pl.BlockSpec((8, bn), lambda j, i: (0, k_gate * nb + j))],
        out_specs=pl.BlockSpec((bm, bn), lambda j, i: (i, j)),
        out_shape=jax.ShapeDtypeStruct((m, n), F32),
        scratch_shapes=[pltpu.VMEM((k, bn), BF16)],
        compiler_params=_params("arbitrary", "arbitrary"),
        name="mm_resid",
    )(a, w, h, mod)


def _diff_attn_kernel(lq1_ref, lk1_ref, lq2_ref, lk2_ref, sub_ref, q_ref, k_ref, v_ref,
                      o_ref, s_a, s_b, m1, l1, a1, m2, l2, a2, *, kc, lambda_init):
    hd = DIFF_HEAD_DIM
    n_chunks = k_ref.shape[0] // kc
    maps = ((m1, l1, a1), (m2, l2, a2))
    for m_ref, l_ref, a_ref in maps:
        m_ref[...] = jnp.full(m_ref.shape, -jnp.inf, F32)
        l_ref[...] = jnp.zeros(l_ref.shape, F32)
        a_ref[...] = jnp.zeros(a_ref.shape, F32)
    dn = (((1,), (1,)), ((), ()))

    def chunk_rows(c):
        return pl.ds(pl.multiple_of(c * kc, kc), kc)

    def scores(c, s_ref, half):
        rows = chunk_rows(c)
        cols = slice(half * hd, (half + 1) * hd)
        s_ref[half] = lax.dot_general(q_ref[:, cols], k_ref[rows, cols], dn, preferred_element_type=F32)

    def update(c, s_ref, half):
        vs = v_ref[chunk_rows(c), :]
        m_ref, l_ref, a_ref = maps[half]
        s = s_ref[half]
        m_old = m_ref[...]
        m_new = jnp.maximum(m_old, jnp.max(s, axis=-1, keepdims=True))
        alpha = jnp.exp2(m_old - m_new)
        p = jnp.exp2(s - m_new)
        l_ref[...] = alpha * l_ref[...] + jnp.sum(p, axis=-1, keepdims=True)
        a_ref[...] = alpha * a_ref[...] + jnp.dot(p.astype(BF16), vs, preferred_element_type=F32)
        m_ref[...] = m_new

    def step(c_next, s_next, c_cur, s_cur):
        for half in range(2):
            if c_next is not None:
                scores(c_next, s_next, half)
            update(c_cur, s_cur, half)

    scores(0, s_a, 0)
    scores(0, s_a, 1)
    n_pairs = (n_chunks - 1) // 2

    def pair(i, carry):
        c = 2 * i
        step(c + 1, s_b, c, s_a)
        step(c + 2, s_a, c + 1, s_b)
        return carry

    lax.fori_loop(0, n_pairs, pair, 0)
    if n_chunks % 2 == 0:
        step(n_chunks - 1, s_b, n_chunks - 2, s_a)
        step(None, None, n_chunks - 1, s_b)
    else:
        step(None, None, n_chunks - 1, s_a)

    lam = (jnp.exp(jnp.sum(lq1_ref[...] * lk1_ref[...], keepdims=True))
           - jnp.exp(jnp.sum(lq2_ref[...] * lk2_ref[...], keepdims=True))) + lambda_init
    o = a1[...] / l1[...] - lam * (a2[...] / l2[...])
    o_ref[...] = (_rms(o) * sub_ref[...] * (1.0 - lambda_init)).astype(o_ref.dtype)


def diff_attn(qkv, lam_vecs, subln, n_heads, q_rows, kv_rows, bq, kc, lambda_init):
    d = n_heads * DIFF_V_DIM
    q0, nq = q_rows
    k0, nk = kv_rows
    assert q0 % bq == 0 and nq % bq == 0 and k0 % nk == 0 and nk % kc == 0
    qb0, kb0 = q0 // bq, k0 // nk
    vec = pl.BlockSpec((1, LANES), lambda h, i: (0, 0))
    in_specs = [vec, vec, vec, vec,
                pl.BlockSpec((1, DIFF_V_DIM), lambda h, i: (0, 0)),
                pl.BlockSpec((bq, DIFF_V_DIM), lambda h, i: (qb0 + i, h)),
                pl.BlockSpec((nk, DIFF_V_DIM), lambda h, i: (kb0, n_heads + h)),
                pl.BlockSpec((nk, DIFF_V_DIM), lambda h, i: (kb0, 2 * n_heads + h))]
    args = [v.reshape(1, LANES) for v in lam_vecs] + [subln.reshape(1, DIFF_V_DIM), qkv, qkv, qkv]
    kern = functools.partial(_diff_attn_kernel, kc=kc, lambda_init=lambda_init)
    return pl.pallas_call(
        kern,
        grid=(n_heads, nq // bq),
        in_specs=in_specs,
        out_specs=pl.BlockSpec((bq, DIFF_V_DIM), lambda h, i: (i, h)),
        out_shape=jax.ShapeDtypeStruct((nq, d), BF16),
        scratch_shapes=[pltpu.VMEM((2, bq, kc), F32), pltpu.VMEM((2, bq, kc), F32),
                        pltpu.VMEM((bq, 1), F32), pltpu.VMEM((bq, 1), F32), pltpu.VMEM((bq, DIFF_V_DIM), F32),
                        pltpu.VMEM((bq, 1), F32), pltpu.VMEM((bq, 1), F32), pltpu.VMEM((bq, DIFF_V_DIM), F32)],
        compiler_params=_params("arbitrary", "arbitrary"),
        name="diff_attn",
    )(*args)


def _retention_kernel(dec_ref, q_ref, k_ref, v_ref, g_ref, kc_ref, vc_ref, *rest, backward, has_prev):
    if has_prev:
        prev_ref, o_ref, s_ref, intra_ref, qd_ref, kd_ref, cd_ref = rest
    else:
        prev_ref = None
        o_ref, s_ref, intra_ref, qd_ref, kd_ref, cd_ref = rest
    c = RET_CHUNK
    n_chunks = q_ref.shape[0] // c
    dn_t = (((0,), (0,)), ((), ()))
    dn_nt = (((1,), (1,)), ((), ()))

    @pl.when(pl.program_id(1) == 0)
    def _():
        lg = jnp.log1p(-jnp.exp(dec_ref[0]))[0:1, 0:1]
        i = lax.broadcasted_iota(I32, (c, c), 0)
        j = lax.broadcasted_iota(I32, (c, c), 1)
        pos = lax.broadcasted_iota(I32, (c, 1), 0).astype(F32)
        rel = (j - i) if backward else (i - j)
        relf = jnp.maximum(rel, 0).astype(F32)
        intra_ref[...] = jnp.where(rel >= 0, jnp.exp(lg * relf), 0.0)
        if backward:
            qd_ref[...] = jnp.exp(lg * (c - pos))
            kd_ref[...] = jnp.exp(lg * pos)
        else:
            qd_ref[...] = jnp.exp(lg * (pos + 1.0))
            kd_ref[...] = jnp.exp(lg * (c - 1.0 - pos))
        cd_ref[...] = jnp.broadcast_to(jnp.exp(lg * float(c)), cd_ref.shape)
        kd = (kc_ref[...].astype(F32) * kd_ref[...]).astype(BF16)
        s_ref[...] = lax.dot_general(kd, vc_ref[...], dn_t, preferred_element_type=F32)

    def body(t, carry):
        cc = (n_chunks - 1 - t) if backward else t
        rows = pl.ds(pl.multiple_of(cc * c, c), c)
        q, k, v = q_ref[rows, :], k_ref[rows, :], v_ref[rows, :]
        scores = lax.dot_general(q, k, dn_nt, preferred_element_type=F32) * intra_ref[...]
        s_old = s_ref[...]
        qd = (q.astype(F32) * qd_ref[...]).astype(BF16)
        o = (jnp.dot(scores.astype(BF16), v, preferred_element_type=F32)
             + jnp.dot(qd, s_old.astype(BF16), preferred_element_type=F32))
        kd = (k.astype(F32) * kd_ref[...]).astype(BF16)
        s_ref[...] = s_old * cd_ref[0:1, 0:1] + lax.dot_general(kd, v, dn_t, preferred_element_type=F32)
        out = _silu(g_ref[rows, :].astype(F32)) * _rms(o)
        if has_prev:
            out = out + prev_ref[rows, :].astype(F32)
        o_ref[rows, :] = out.astype(o_ref.dtype)
        return carry

    lax.fori_loop(0, n_chunks, body, 0, unroll=2)


def retention_dir(proj, decay, n_heads, t_lat, backward, prev=None):
    c = RET_CHUNK
    d = n_heads * RET_QK_DIM
    t_ctx = proj.shape[0] - t_lat
    assert t_ctx == c and t_lat % c == 0
    rows = _largest_divisor(t_lat, (2048, 1024, 512, 256))
    n_steps = t_lat // rows
    step = (lambda s: n_steps - 1 - s) if backward else (lambda s: s)
    qb, vb = d // RET_QK_DIM, 2 * d // RET_V_DIM
    gb = (6 if backward else 4) * d // RET_V_DIM
    ctx_blk = t_lat // c
    in_specs = [pl.BlockSpec((1, 8, LANES), lambda h, s: (h, 0, 0)),
                pl.BlockSpec((rows, RET_QK_DIM), lambda h, s: (step(s), h)),
                pl.BlockSpec((rows, RET_QK_DIM), lambda h, s: (step(s), qb + h)),
                pl.BlockSpec((rows, RET_V_DIM), lambda h, s: (step(s), vb + h)),
                pl.BlockSpec((rows, RET_V_DIM), lambda h, s: (step(s), gb + h)),
                pl.BlockSpec((c, RET_QK_DIM), lambda h, s: (ctx_blk, qb + h)),
                pl.BlockSpec((c, RET_V_DIM), lambda h, s: (ctx_blk, vb + h))]
    args = [jnp.broadcast_to(decay.astype(F32)[:, None, None], (n_heads, 8, LANES)), proj, proj, proj, proj, proj, proj]
    if prev is not None:
        in_specs.append(pl.BlockSpec((rows, RET_V_DIM), lambda h, s: (step(s), h)))
        args.append(prev)
    kern = functools.partial(_retention_kernel, backward=backward, has_prev=prev is not None)
    return pl.pallas_call(
        kern,
        grid=(n_heads, n_steps),
        in_specs=in_specs,
        out_specs=pl.BlockSpec((rows, RET_V_DIM), lambda h, s: (step(s), h)),
        out_shape=jax.ShapeDtypeStruct((t_lat, n_heads * RET_V_DIM), BF16),
        scratch_shapes=[pltpu.VMEM((RET_QK_DIM, RET_V_DIM), F32), pltpu.VMEM((c, c), F32),
                        pltpu.VMEM((c, 1), F32), pltpu.VMEM((c, 1), F32), pltpu.VMEM((8, LANES), F32)],
        compiler_params=_params("arbitrary", "arbitrary"),
        name="retention_bwd" if backward else "retention_fwd",
    )(*args)


def _norm_logits_kernel(h_ref, gain_ref, sh_ref, sc_ref, wr_ref, lg_ref, *, n_lat_tiles):
    is_ctx = pl.program_id(0) >= n_lat_tiles
    y = _rms(h_ref[...]) * gain_ref[...]
    n2 = (y * (1.0 + _mod_rows(sc_ref, is_ctx)) + _mod_rows(sh_ref, is_ctx)).astype(BF16)
    lg_ref[...] = jnp.dot(n2, wr_ref[...].astype(BF16), preferred_element_type=F32)


def norm_logits(h, gain, mod, k_shift, k_scale, w_router, li, t_lat):
    rows, d = h.shape
    e = w_router.shape[2]
    return pl.pallas_call(
        functools.partial(_norm_logits_kernel, n_lat_tiles=t_lat // ROW_TILE),
        grid=(rows // ROW_TILE,),
        in_specs=[pl.BlockSpec((ROW_TILE, d), lambda i: (i, 0)),
                  pl.BlockSpec((1, d), lambda i: (0, 0)),
                  pl.BlockSpec((8, d), lambda i: (0, k_shift)),
                  pl.BlockSpec((8, d), lambda i: (0, k_scale)),
                  pl.BlockSpec((None, d, e), lambda i: (li, 0, 0))],
        out_specs=pl.BlockSpec((ROW_TILE, e), lambda i: (i, 0)),
        out_shape=jax.ShapeDtypeStruct((rows, e), F32),
        compiler_params=_params("arbitrary"),
        name="norm_logits",
    )(h, gain.reshape(1, d), mod, mod, w_router)


def _route_kernel(lg_ref, idx_ref, gate_ref, bits_ref, key_ref, w_ref, *, cap):
    t, e = lg_ref.shape
    pb = min(t, 256)
    tb = min(t, 1024)
    lg = lg_ref[...]
    ex = jnp.exp(lg - jnp.max(lg, axis=-1, keepdims=True))
    aff = ex / jnp.sum(ex, axis=-1, keepdims=True)
    bits_ref[...] = lax.bitcast_convert_type(aff, I32)
    ahi = aff.astype(BF16)
    r1 = aff - ahi.astype(F32)
    amid = r1.astype(BF16)
    alo = (r1 - amid.astype(F32)).astype(BF16)
    tok = lax.broadcasted_iota(I32, (t, e), 0)
    lane = lax.broadcasted_iota(I32, (t, e), 1)
    tdig = jnp.where(lane == 0, tok >> 7, jnp.where(lane == 1, tok & 127, 0)).astype(F32).astype(BF16)
    pr = lax.broadcasted_iota(I32, (e, 4 * e), 0)
    pc = lax.broadcasted_iota(I32, (e, 4 * e), 1)
    packed = jnp.zeros((t, 4 * e), F32)
    for piece, part in enumerate((ahi, amid, alo, tdig)):
        place = jnp.where(pc == pr + piece * e, 1.0, 0.0).astype(BF16)
        packed = packed + jnp.dot(part, place, preferred_element_type=F32)
    w_ref[...] = packed.astype(BF16)

    def thr_body(it, thr):
        cand = thr | (jnp.int32(1) << (30 - it))
        cnt = jnp.sum(jnp.where(bits_ref[...] >= cand, 1.0, 0.0), axis=0, keepdims=True)
        return jnp.where(cnt >= cap, cand, thr)

    thr = lax.fori_loop(0, 31, thr_body, jnp.zeros((1, e), I32))
    n_gt = jnp.sum(jnp.where(bits_ref[...] > thr, 1.0, 0.0), axis=0, keepdims=True)
    need = cap - n_gt

    ri = lax.broadcasted_iota(I32, (pb, pb), 0)
    ci = lax.broadcasted_iota(I32, (pb, pb), 1)
    ltri = jnp.where(ci < ri, 1.0, 0.0).astype(BF16)

    def pre_body(b, carry):
        ceq, csel = carry
        rows = pl.ds(pl.multiple_of(b * pb, pb), pb)
        bt = bits_ref[rows, :]
        gtb = bt > thr
        eqf = jnp.where(bt == thr, 1.0, 0.0)
        eq_rank = jnp.dot(ltri, eqf.astype(BF16), preferred_element_type=F32) + ceq
        self_ = jnp.where(gtb, 1.0, jnp.where(eq_rank < need, eqf, 0.0))
        pos = jnp.dot(ltri, self_.astype(BF16), preferred_element_type=F32) + csel
        key_ref[rows, :] = jnp.where(self_ > 0.0, pos, -1.0).astype(I32)
        return (ceq + jnp.sum(eqf, axis=0, keepdims=True), csel + jnp.sum(self_, axis=0, keepdims=True))

    zero = jnp.zeros((1, e), F32)
    lax.fori_loop(0, t // pb, pre_body, (zero, zero))

    slot = lax.broadcasted_iota(I32, (1, cap), 1)
    dn_t = (((0,), (0,)), ((), ()))
    for ex_i in range(e):
        def slot_body(b, carry, ex_i=ex_i):
            rows = pl.ds(pl.multiple_of(b * tb, tb), tb)
            onehot = jnp.where(key_ref[rows, ex_i:ex_i + 1] == slot, 1.0, 0.0).astype(BF16)
            return carry + lax.dot_general(w_ref[rows, :], onehot, dn_t, preferred_element_type=F32)

        acc = lax.fori_loop(0, t // tb, slot_body, jnp.zeros((4 * e, cap), F32))
        gate_ref[ex_i:ex_i + 1, :] = ((acc[ex_i:ex_i + 1, :] + acc[e + ex_i:e + ex_i + 1, :])
                                      + acc[2 * e + ex_i:2 * e + ex_i + 1, :])
        idx_ref[ex_i:ex_i + 1, :] = (acc[3 * e:3 * e + 1, :] * 128.0 + acc[3 * e + 1:3 * e + 2, :]).astype(I32)


def route(logits, cap):
    t, e = logits.shape
    return pl.pallas_call(
        functools.partial(_route_kernel, cap=cap),
        out_shape=(jax.ShapeDtypeStruct((e, cap), I32), jax.ShapeDtypeStruct((e, cap), F32)),
        scratch_shapes=[pltpu.VMEM((t, e), I32), pltpu.VMEM((t, e), I32), pltpu.VMEM((t, 4 * e), BF16)],
        compiler_params=pltpu.CompilerParams(vmem_limit_bytes=VMEM_LIMIT),
        name="route",
    )(logits)


def _issue_rows(n, fn):
    lax.fori_loop(0, n, lambda r, carry: (fn(r), carry)[1], 0, unroll=DMA_UNROLL)


def _gather_kernel(idx_sm, h_hbm, gain_ref, sh_ref, sc_ref, o_ref, buf, rinv_ref, mul_ref, sh_b_ref, sems,
                   *, sb, n_lat_slots):
    base = pl.program_id(0) * buf.shape[0]
    n_sub = buf.shape[0] // sb

    def row_copy(r, tok, j):
        return pltpu.make_async_copy(h_hbm.at[pl.ds(tok, 1), :], buf.at[pl.ds(r, 1), :], sems.at[j])

    def start_sub(j, carry):
        _issue_rows(sb, lambda r: row_copy(j * sb + r, idx_sm[base + j * sb + r], j).start())
        return carry

    lax.fori_loop(0, n_sub, start_sub, 0)
    for kind in range(2):
        mul_ref[kind] = jnp.broadcast_to(gain_ref[...] * (1.0 + sc_ref[kind:kind + 1, :]), mul_ref.shape[1:])
        sh_b_ref[kind] = jnp.broadcast_to(sh_ref[kind:kind + 1, :], sh_b_ref.shape[1:])

    def finish_sub(j, carry):
        _issue_rows(sb, lambda r: row_copy(j * sb + r, 0, j).wait())
        xs = buf[pl.ds(pl.multiple_of(j * sb, sb), sb), :]
        rinv_ref[...] = lax.rsqrt(jnp.mean(xs * xs, axis=-1, keepdims=True) + NORM_EPS)

        def group(g, c2):
            r0 = pl.multiple_of(j * sb + g * ROW_GROUP, ROW_GROUP)
            rows = pl.ds(r0, ROW_GROUP)
            kind = (r0 >= n_lat_slots).astype(I32)
            y = buf[rows, :] * rinv_ref[pl.ds(pl.multiple_of(g * ROW_GROUP, ROW_GROUP), ROW_GROUP), :]
            o_ref[rows, :] = (y * mul_ref[kind] + sh_b_ref[kind]).astype(o_ref.dtype)
            return c2

        lax.fori_loop(0, sb // ROW_GROUP, group, 0, unroll=True)
        return carry

    lax.fori_loop(0, n_sub, finish_sub, 0)


def moe_gather(h, idx, gain, mod, k_shift, k_scale, n_lat_slots, sb):
    e, ct = idx.shape
    d = h.shape[1]
    assert n_lat_slots % ROW_GROUP == 0
    grid_spec = pltpu.PrefetchScalarGridSpec(
        num_scalar_prefetch=1,
        grid=(e,),
        in_specs=[pl.BlockSpec(memory_space=pl.ANY),
                  pl.BlockSpec((1, d), lambda i, idx_sm: (0, 0)),
                  pl.BlockSpec((8, d), lambda i, idx_sm: (0, k_shift)),
                  pl.BlockSpec((8, d), lambda i, idx_sm: (0, k_scale))],
        out_specs=pl.BlockSpec((None, ct, d), lambda i, idx_sm: (i, 0, 0)),
        scratch_shapes=[pltpu.VMEM((ct, d), F32), pltpu.VMEM((sb, 1), F32),
                        pltpu.VMEM((2, ROW_GROUP, d), F32), pltpu.VMEM((2, ROW_GROUP, d), F32),
                        pltpu.SemaphoreType.DMA((ct // sb,))],
    )
    return pl.pallas_call(
        functools.partial(_gather_kernel, sb=sb, n_lat_slots=n_lat_slots),
        grid_spec=grid_spec,
        out_shape=jax.ShapeDtypeStruct((e, ct, d), BF16),
        compiler_params=_params("arbitrary"),
        name="moe_gather",
    )(idx.reshape(-1), h, gain.reshape(1, d), mod, mod)


def _ffn_up_kernel(x_ref, wg_ref, wu_ref, o_ref):
    x = x_ref[...]
    g = jnp.dot(x, wg_ref[...].astype(BF16), preferred_element_type=F32)
    u = jnp.dot(x, wu_ref[...].astype(BF16), preferred_element_type=F32)
    o_ref[...] = (_silu(g) * u).astype(o_ref.dtype)


def ffn_up(xg, w_gate, w_up, li):
    e, ct, d = xg.shape
    f = w_gate.shape[3]
    fb = 256
    return pl.pallas_call(
        _ffn_up_kernel,
        grid=(e, f // fb),
        in_specs=[pl.BlockSpec((None, ct, d), lambda i, j: (i, 0, 0)),
                  pl.BlockSpec((None, None, d, fb), lambda i, j: (li, i, 0, j)),
                  pl.BlockSpec((None, None, d, fb), lambda i, j: (li, i, 0, j))],
        out_specs=pl.BlockSpec((None, ct, fb), lambda i, j: (i, 0, j)),
        out_shape=jax.ShapeDtypeStruct((e, ct, f), BF16),
        compiler_params=_params("arbitrary", "arbitrary"),
        name="ffn_up",
    )(xg, w_gate, w_up)


def _ffn_down_kernel(h_ref, wd_ref, gate_ref, g2_ref, o_ref, *, n_lat_slots):
    acc = jnp.dot(h_ref[...], wd_ref[...].astype(BF16), preferred_element_type=F32)
    is_ctx = lax.broadcasted_iota(I32, (acc.shape[0], 1), 0) >= n_lat_slots
    g2 = jnp.where(is_ctx, g2_ref[1:2, :], g2_ref[0:1, :])
    o_ref[...] = (acc * gate_ref[...] * g2).astype(o_ref.dtype)


def ffn_down(hid, w_down, li, gate, mod, k_gate, n_lat_slots):
    e, ct, f = hid.shape
    d = w_down.shape[3]
    nb = _largest_divisor(d, (2048, 1024, 512, 256, 128))
    nblk = d // nb
    return pl.pallas_call(
        functools.partial(_ffn_down_kernel, n_lat_slots=n_lat_slots),
        grid=(e, nblk),
        in_specs=[pl.BlockSpec((None, ct, f), lambda i, j: (i, 0, 0)),
                  pl.BlockSpec((None, None, f, nb), lambda i, j: (li, i, 0, j)),
                  pl.BlockSpec((None, ct, 1), lambda i, j: (i, 0, 0)),
                  pl.BlockSpec((8, nb), lambda i, j: (0, k_gate * nblk + j))],
        out_specs=pl.BlockSpec((None, ct, nb), lambda i, j: (i, 0, j)),
        out_shape=jax.ShapeDtypeStruct((e, ct, d), BF16),
        compiler_params=_params("arbitrary", "arbitrary"),
        name="ffn_down",
    )(hid, w_down, gate[:, :, None], mod)


def _scatter_kernel(idx_sm, y_ref, acc_in, acc_out, buf, sems_in, sem_out, *, sb):
    del acc_in
    ct = buf.shape[0]
    base = pl.program_id(0) * ct
    n_sub = ct // sb

    def fetch(r, tok, j):
        return pltpu.make_async_copy(acc_out.at[pl.ds(tok, 1), :], buf.at[pl.ds(r, 1), :], sems_in.at[j])

    def put(r, tok):
        return pltpu.make_async_copy(buf.at[pl.ds(r, 1), :], acc_out.at[pl.ds(tok, 1), :], sem_out)

    def start_sub(j, carry):
        _issue_rows(sb, lambda r: fetch(j * sb + r, idx_sm[base + j * sb + r], j).start())
        return carry

    lax.fori_loop(0, n_sub, start_sub, 0)

    def finish_sub(j, carry):
        _issue_rows(sb, lambda r: fetch(j * sb + r, 0, j).wait())
        rows = pl.ds(pl.multiple_of(j * sb, sb), sb)
        buf[rows, :] = buf[rows, :] + y_ref[rows, :].astype(F32)
        _issue_rows(sb, lambda r: put(j * sb + r, idx_sm[base + j * sb + r]).start())
        return carry

    lax.fori_loop(0, n_sub, finish_sub, 0)
    _issue_rows(ct, lambda r: put(r, 0).wait())


def moe_scatter_add(acc, idx, y, sb):
    e, ct = idx.shape
    rows, d = acc.shape
    grid_spec = pltpu.PrefetchScalarGridSpec(
        num_scalar_prefetch=1,
        grid=(e,),
        in_specs=[pl.BlockSpec((None, ct, d), lambda i, idx_sm: (i, 0, 0)),
                  pl.BlockSpec(memory_space=pl.ANY)],
        out_specs=pl.BlockSpec(memory_space=pl.ANY),
        scratch_shapes=[pltpu.VMEM((ct, d), F32), pltpu.SemaphoreType.DMA((ct // sb,)),
                        pltpu.SemaphoreType.DMA(())],
    )
    return pl.pallas_call(
        functools.partial(_scatter_kernel, sb=sb),
        grid_spec=grid_spec,
        out_shape=jax.ShapeDtypeStruct((rows, d), F32),
        input_output_aliases={2: 0},
        compiler_params=_params("arbitrary"),
        name="moe_scatter_add",
    )(idx.reshape(-1), y, acc)


def expert_choice_moe(h, gain, mod, w_router, w_gate, w_up, w_down, li, groups, t_lat):
    e = w_router.shape[2]
    logits = norm_logits(h, gain, mod, 3, 4, w_router, li, t_lat)
    idx_parts, gate_parts = [], []
    for start, count in groups:
        idx_g, gate_g = route(logits[start:start + count], EC_CAPACITY_FACTOR * count // e)
        idx_parts.append(idx_g + start)
        gate_parts.append(gate_g)
    idx = jnp.concatenate(idx_parts, axis=1)
    gate = jnp.concatenate(gate_parts, axis=1)
    ct = idx.shape[1]
    sb = _largest_divisor(ct, (128, 96, 64, 32, 16))
    n_lat_slots = EC_CAPACITY_FACTOR * t_lat // e
    xg = moe_gather(h, idx, gain, mod, 3, 4, n_lat_slots, sb)
    hid = ffn_up(xg, w_gate, w_up, li)
    y = ffn_down(hid, w_down, li, gate, mod, 5, n_lat_slots)
    return moe_scatter_add(h, idx, y, sb)


def _axial_tables(t_lat, t_ctx, rot_dim):
    axis_dim = rot_dim // 2
    tok = jnp.arange(t_lat)
    row = (tok // GRID_W).astype(F32)
    col = (tok % GRID_W).astype(F32)
    inv_freq = ROPE_BASE ** (-jnp.arange(0, axis_dim, 2, dtype=F32) / axis_dim)
    ar, ac = row[:, None] * inv_freq, col[:, None] * inv_freq
    cos = jnp.concatenate([jnp.cos(ar), jnp.cos(ar), jnp.cos(ac), jnp.cos(ac)], axis=-1)
    sin = jnp.concatenate([-jnp.sin(ar), jnp.sin(ar), -jnp.sin(ac), jnp.sin(ac)], axis=-1)
    cos = jnp.concatenate([cos, jnp.ones((t_ctx, rot_dim), F32)], axis=0)
    sin = jnp.concatenate([sin, jnp.zeros((t_ctx, rot_dim), F32)], axis=0)
    return cos, sin


def kernel(x, c, ctx, c_ctx, ada_w, ada_b, norm_mix, norm_ffn, diff_w_in, diff_w_out, diff_q_norm, diff_k_norm, diff_lambda_q1, diff_lambda_k1, diff_lambda_q2, diff_lambda_k2, diff_subln, ret_w_in, ret_w_out, ret_decay_fwd, ret_decay_bwd, router_w, expert_w_gate, expert_w_up, expert_w_down):
    batch, t_lat, d = x.shape
    t_ctx = ctx.shape[1]
    depth = ada_w.shape[0]
    assert batch == 1 and depth == 2 and t_lat % ROW_TILE == 0 and t_ctx == ROW_TILE
    t_all = t_lat + t_ctx
    diff_heads = d // DIFF_V_DIM
    ret_heads = d // RET_QK_DIM

    h = jnp.concatenate([x[0], ctx[0]], axis=0)
    cvec = jnp.zeros((8, d), F32).at[0].set(c[0]).at[1].set(c_ctx)
    cos_d, sin_d = _axial_tables(t_lat, t_ctx, DIFF_HEAD_DIM)
    cos_r, sin_r = _axial_tables(t_lat, t_ctx, RET_QK_DIM)

    mod = adaln(cvec, ada_w, ada_b, 0)
    n1 = norm_mod(h, norm_mix[0], mod, 0, 1, t_lat)
    qkv = mm_diff_in(n1, diff_w_in, 0, diff_q_norm[0], diff_k_norm[0], cos_d, sin_d)
    lam_vecs = (diff_lambda_q1[0], diff_lambda_k1[0], diff_lambda_q2[0], diff_lambda_k2[0])
    lambda_init = 0.8 - 0.6 * math.exp(-0.3 * 0)
    bq = _largest_divisor(t_lat, (512, 256))
    kc = _largest_divisor(t_all, (2816, 1408, 768, 256))
    att_lat = diff_attn(qkv, lam_vecs, diff_subln[0], diff_heads, (0, t_lat), (0, t_all), bq, kc, lambda_init)
    att_ctx = diff_attn(qkv, lam_vecs, diff_subln[0], diff_heads, (t_lat, t_ctx), (t_lat, t_ctx), t_ctx, t_ctx,
                        lambda_init)
    h = mm_resid(jnp.concatenate([att_lat, att_ctx], axis=0), diff_w_out, 0, h, mod, 2, t_lat)
    h = expert_choice_moe(h, norm_ffn[0], mod, router_w, expert_w_gate, expert_w_up, expert_w_down, 0,
                          [(0, t_lat), (t_lat, t_ctx)], t_lat)

    mod = adaln(cvec, ada_w, ada_b, 1)
    n1 = norm_mod(h, norm_mix[1], mod, 0, 1, t_lat)
    proj = mm_ret_in(n1, ret_w_in, 0, cos_r, sin_r)
    gated = retention_dir(proj, ret_decay_fwd[0], ret_heads, t_lat, backward=False)
    gated = retention_dir(proj, ret_decay_bwd[0], ret_heads, t_lat, backward=True, prev=gated)
    h = mm_resid(gated, ret_w_out, 0, h, mod, 2, t_lat, single_buffer_w=True)
    h = expert_choice_moe(h, norm_ffn[1], mod, router_w, expert_w_gate, expert_w_up, expert_w_down, 1,
                          [(0, t_lat)], t_lat)
    return h[None]
```

```python
import functools
import math

import jax
import jax.numpy as jnp
from jax import lax
from jax.experimental import pallas as pl
from jax.experimental.pallas import tpu as pltpu

F32 = jnp.float32
BF16 = jnp.bfloat16
I32 = jnp.int32

NORM_EPS = 1e-6
GRID_W = 64
ROPE_BASE = 10000.0
DIFF_HEAD_DIM = 128
DIFF_V_DIM = 2 * DIFF_HEAD_DIM
RET_QK_DIM = 256
RET_V_DIM = 2 * RET_QK_DIM
RET_CHUNK = 256
EC_CAPACITY_FACTOR = 2
ROW_TILE = 256
ROW_GROUP = 16
DMA_UNROLL = 8
MM_PART_ROWS = (256, 352)
LANES = 128
V7X_VMEM_BYTES = 64 * 1024 * 1024
VMEM_LIMIT = V7X_VMEM_BYTES - 3 * 1024 * 1024


def _params(*sem):
    return pltpu.CompilerParams(dimension_semantics=sem, vmem_limit_bytes=VMEM_LIMIT)


def _silu(x):
    half = 0.5 * x
    return half + half * jnp.tanh(half)


def _rms(x):
    return x * lax.rsqrt(jnp.mean(x * x, axis=-1, keepdims=True) + NORM_EPS)


def _largest_divisor(n, candidates):
    for cand in candidates:
        if n % cand == 0:
            return cand
    raise ValueError(f"no tile in {candidates} divides {n}")


def _mm_tiles(m, k, n_unit, f32_rows=False):
    if k > 4096:
        return _largest_divisor(m, (512, 256)), 256
    if f32_rows:
        return _largest_divisor(m, (1408, 1024, 768, 512, 256)), 512
    return _largest_divisor(m, (1024, 768, 512, 256)), _largest_divisor(n_unit, (1024, 512))


def _adaln_kernel(c_ref, w_ref, b_ref, o_ref):
    s = _silu(c_ref[...])
    o_ref[...] = jnp.dot(s.astype(BF16), w_ref[...].astype(BF16), preferred_element_type=F32) + b_ref[...]


def adaln(cvec, w, b, li):
    _, d, n = w.shape
    bn = 512
    return pl.pallas_call(
        _adaln_kernel,
        grid=(n // bn,),
        in_specs=[pl.BlockSpec((8, d), lambda j: (0, 0)),
                  pl.BlockSpec((None, d, bn), lambda j: (li, 0, j)),
                  pl.BlockSpec((None, 1, bn), lambda j: (li, 0, j))],
        out_specs=pl.BlockSpec((8, bn), lambda j: (0, j)),
        out_shape=jax.ShapeDtypeStruct((8, n), F32),
        compiler_params=_params("arbitrary"),
        name="adaln",
    )(cvec, w, b[:, None, :])


def _mod_rows(mod_ref, is_ctx):
    return jnp.where(is_ctx, mod_ref[1:2, :], mod_ref[0:1, :])


def _norm_mod_kernel(h_ref, gain_ref, sh_ref, sc_ref, o_ref, *, n_lat_tiles):
    is_ctx = pl.program_id(0) >= n_lat_tiles
    y = _rms(h_ref[...]) * gain_ref[...]
    o_ref[...] = (y * (1.0 + _mod_rows(sc_ref, is_ctx)) + _mod_rows(sh_ref, is_ctx)).astype(o_ref.dtype)


def norm_mod(h, gain, mod, k_shift, k_scale, t_lat):
    rows, d = h.shape
    return pl.pallas_call(
        functools.partial(_norm_mod_kernel, n_lat_tiles=t_lat // ROW_TILE),
        grid=(rows // ROW_TILE,),
        in_specs=[pl.BlockSpec((ROW_TILE, d), lambda i: (i, 0)),
                  pl.BlockSpec((1, d), lambda i: (0, 0)),
                  pl.BlockSpec((8, d), lambda i: (0, k_shift)),
                  pl.BlockSpec((8, d), lambda i: (0, k_scale))],
        out_specs=pl.BlockSpec((ROW_TILE, d), lambda i: (i, 0)),
        out_shape=jax.ShapeDtypeStruct((rows, d), BF16),
        compiler_params=_params("arbitrary"),
        name="norm_mod",
    )(h, gain.reshape(1, d), mod, mod)


def _mm_parts(a_ref, w_ref, wbf_ref):
    @pl.when(pl.program_id(1) == 0)
    def _():
        wbf_ref[...] = w_ref[...].astype(BF16)

    bm = a_ref.shape[0]
    part = _largest_divisor(bm, MM_PART_ROWS)
    for r in range(bm // part):
        rows = slice(r * part, (r + 1) * part)
        yield rows, jnp.dot(a_ref[rows, :], wbf_ref[...], preferred_element_type=F32)


def _rot_half_64(x):
    lane = lax.broadcasted_iota(I32, (1, LANES), 1)
    first = (lane % 64) < 32
    return jnp.where(first, pltpu.roll(x, 96, 1), pltpu.roll(x, 32, 1))


def _mm_diff_in_kernel(a_ref, w_ref, qg_ref, kg_ref, cos_ref, sin_ref, o_ref, wbf_ref, *, nq, nqk, qscale):
    j = pl.program_id(0)
    is_q, is_qk = j < nq, j < nqk
    gain = jnp.where(is_q, qg_ref[...], kg_ref[...])
    scale = jnp.where(is_q, qscale, 1.0)
    for rows, acc in _mm_parts(a_ref, w_ref, wbf_ref):
        cos, sin = cos_ref[rows, :], sin_ref[rows, :]
        for cidx in range(acc.shape[1] // LANES):
            sl = slice(cidx * LANES, (cidx + 1) * LANES)
            x = acc[:, sl]
            y = _rms(x) * gain
            y = (y * cos + _rot_half_64(y) * sin) * scale
            o_ref[rows, sl] = jnp.where(is_qk, y, x).astype(o_ref.dtype)


def mm_diff_in(a, w, li, q_gain, k_gain, cos, sin):
    m, k = a.shape
    n = w.shape[2]
    d = n // 3
    bm, bn = _mm_tiles(m, k, d)
    nqk = 2 * d // bn
    kern = functools.partial(_mm_diff_in_kernel, nq=d // bn, nqk=nqk,
                             qscale=DIFF_HEAD_DIM ** -0.5 * math.log2(math.e))
    return pl.pallas_call(
        kern,
        grid=(n // bn, m // bm),
        in_specs=[pl.BlockSpec((bm, k), lambda j, i: (i, 0)),
                  pl.BlockSpec((None, k, bn), lambda j, i: (li, 0, j)),
                  pl.BlockSpec((1, LANES), lambda j, i: (0, 0)),
                  pl.BlockSpec((1, LANES), lambda j, i: (0, 0)),
                  pl.BlockSpec((bm, LANES), lambda j, i: (jnp.where(j < nqk, i, 0), 0)),
                  pl.BlockSpec((bm, LANES), lambda j, i: (jnp.where(j < nqk, i, 0), 0))],
        out_specs=pl.BlockSpec((bm, bn), lambda j, i: (i, j)),
        out_shape=jax.ShapeDtypeStruct((m, n), BF16),
        scratch_shapes=[pltpu.VMEM((k, bn), BF16)],
        compiler_params=_params("arbitrary", "arbitrary"),
        name="mm_diff_in",
    )(a, w, q_gain.reshape(1, LANES), k_gain.reshape(1, LANES), cos, sin)


def _mm_ret_in_kernel(a_ref, w_ref, cos_ref, sin_ref, o_ref, wbf_ref, *, nq, nqk, kscale):
    j = pl.program_id(0)
    is_qk = j < nqk
    scale = jnp.where(j < nq, 1.0, kscale)
    for rows, acc in _mm_parts(a_ref, w_ref, wbf_ref):
        for cidx in range(acc.shape[1] // LANES):
            sl = slice(cidx * LANES, (cidx + 1) * LANES)
            tl = slice((cidx % 2) * LANES, (cidx % 2 + 1) * LANES)
            x = acc[:, sl]
            y = (x * cos_ref[rows, tl] + pltpu.roll(x, 64, 1) * sin_ref[rows, tl]) * scale
            o_ref[rows, sl] = jnp.where(is_qk, y, x).astype(o_ref.dtype)


def mm_ret_in(a, w, li, cos, sin):
    m, k = a.shape
    n = w.shape[2]
    d = n // 8
    bm, bn = _mm_tiles(m, k, d)
    nqk = 2 * d // bn
    kern = functools.partial(_mm_ret_in_kernel, nq=d // bn, nqk=nqk, kscale=RET_QK_DIM ** -0.5)
    return pl.pallas_call(
        kern,
        grid=(n // bn, m // bm),
        in_specs=[pl.BlockSpec((bm, k), lambda j, i: (i, 0)),
                  pl.BlockSpec((None, k, bn), lambda j, i: (li, 0, j)),
                  pl.BlockSpec((bm, RET_QK_DIM), lambda j, i: (jnp.where(j < nqk, i, 0), 0)),
                  pl.BlockSpec((bm, RET_QK_DIM), lambda j, i: (jnp.where(j < nqk, i, 0), 0))],
        out_specs=pl.BlockSpec((bm, bn), lambda j, i: (i, j)),
        out_shape=jax.ShapeDtypeStruct((m, n), BF16),
        scratch_shapes=[pltpu.VMEM((k, bn), BF16)],
        compiler_params=_params("arbitrary", "arbitrary"),
        name="mm_ret_in",
    )(a, w, cos, sin)


def _mm_resid_kernel(a_ref, w_ref, h_ref, g_ref, o_ref, wbf_ref, *, t_lat):
    bm = a_ref.shape[0]
    for rows, acc in _mm_parts(a_ref, w_ref, wbf_ref):
        row = pl.program_id(1) * bm + rows.start + lax.broadcasted_iota(I32, (acc.shape[0], 1), 0)
        gate = jnp.where(row >= t_lat, g_ref[1:2, :], g_ref[0:1, :])
        o_ref[rows, :] = h_ref[rows, :] + gate * acc


def mm_resid(a, w, li, h, mod, k_gate, t_lat, wide_cols=False):
    m, k = a.shape
    n = w.shape[2]
    bm, bn = _mm_tiles(m, k, n, f32_rows=True)
    if wide_cols:
        bn = 512
    w_spec = pl.BlockSpec((None, k, bn), lambda j, i: (li, 0, j))
    nb = n // bn
    return pl.pallas_call(
        functools.partial(_mm_resid_kernel, t_lat=t_lat),
        grid=(nb, m // bm),
        in_specs=[pl.BlockSpec((bm, k), lambda j, i: (i, 0)),
                  w_spec,
                  pl.BlockSpec((bm, bn), lambda j, i: (i, j)),
                  pl.BlockSpec((8, bn), lambda j, i: (0, k_gate * nb + j))],
        out_specs=pl.BlockSpec((bm, bn), lambda j, i: (i, j)),
        out_shape=jax.ShapeDtypeStruct((m, n), F32),
        scratch_shapes=[pltpu.VMEM((k, bn), BF16)],
        compiler_params=_params("arbitrary", "arbitrary"),
        name="mm_resid",
    )(a, w, h, mod)


def _diff_attn_kernel(lq1_ref, lk1_ref, lq2_ref, lk2_ref, sub_ref, q_ref, k_ref, v_ref,
                      o_ref, s_a, s_b, m1, l1, a1, m2, l2, a2, *, kc, lambda_init):
    hd = DIFF_HEAD_DIM
    n_chunks = k_ref.shape[0] // kc
    maps = ((m1, l1, a1), (m2, l2, a2))
    for m_ref, l_ref, a_ref in maps:
        m_ref[...] = jnp.full(m_ref.shape, -jnp.inf, F32)
        l_ref[...] = jnp.zeros(l_ref.shape, F32)
        a_ref[...] = jnp.zeros(a_ref.shape, F32)
    dn = (((1,), (1,)), ((), ()))

    def chunk_rows(c):
        return pl.ds(pl.multiple_of(c * kc, kc), kc)

    def scores(c, s_ref, half):
        rows = chunk_rows(c)
        cols = slice(half * hd, (half + 1) * hd)
        s_ref[half] = lax.dot_general(q_ref[:, cols], k_ref[rows, cols], dn, preferred_element_type=F32)

    def update(c, s_ref, half):
        vs = v_ref[chunk_rows(c), :]
        m_ref, l_ref, a_ref = maps[half]
        s = s_ref[half]
        m_old = m_ref[...]
        m_new = jnp.maximum(m_old, jnp.max(s, axis=-1, keepdims=True))
        alpha = jnp.exp2(m_old - m_new)
        p = jnp.exp2(s - m_new)
        l_ref[...] = alpha * l_ref[...] + jnp.sum(p, axis=-1, keepdims=True)
        a_ref[...] = alpha * a_ref[...] + jnp.dot(p.astype(BF16), vs, preferred_element_type=F32)
        m_ref[...] = m_new

    def step(c_next, s_next, c_cur, s_cur):
        for half in range(2):
            if c_next is not None:
                scores(c_next, s_next, half)
            update(c_cur, s_cur, half)

    scores(0, s_a, 0)
    scores(0, s_a, 1)
    n_pairs = (n_chunks - 1) // 2

    def pair(i, carry):
        c = 2 * i
        step(c + 1, s_b, c, s_a)
        step(c + 2, s_a, c + 1, s_b)
        return carry

    lax.fori_loop(0, n_pairs, pair, 0)
    if n_chunks % 2 == 0:
        step(n_chunks - 1, s_b, n_chunks - 2, s_a)
        step(None, None, n_chunks - 1, s_b)
    else:
        step(None, None, n_chunks - 1, s_a)

    lam = (jnp.exp(jnp.sum(lq1_ref[...] * lk1_ref[...], keepdims=True))
           - jnp.exp(jnp.sum(lq2_ref[...] * lk2_ref[...], keepdims=True))) + lambda_init
    o = a1[...] / l1[...] - lam * (a2[...] / l2[...])
    o_ref[...] = (_rms(o) * sub_ref[...] * (1.0 - lambda_init)).astype(o_ref.dtype)


def diff_attn(qkv, lam_vecs, subln, n_heads, q_rows, kv_rows, bq, kc, lambda_init):
    d = n_heads * DIFF_V_DIM
    q0, nq = q_rows
    k0, nk = kv_rows
    assert q0 % bq == 0 and nq % bq == 0 and k0 % nk == 0 and nk % kc == 0
    qb0, kb0 = q0 // bq, k0 // nk
    vec = pl.BlockSpec((1, LANES), lambda h, i: (0, 0))
    in_specs = [vec, vec, vec, vec,
                pl.BlockSpec((1, DIFF_V_DIM), lambda h, i: (0, 0)),
                pl.BlockSpec((bq, DIFF_V_DIM), lambda h, i: (qb0 + i, h)),
                pl.BlockSpec((nk, DIFF_V_DIM), lambda h, i: (kb0, n_heads + h)),
                pl.BlockSpec((nk, DIFF_V_DIM), lambda h, i: (kb0, 2 * n_heads + h))]
    args = [v.reshape(1, LANES) for v in lam_vecs] + [subln.reshape(1, DIFF_V_DIM), qkv, qkv, qkv]
    kern = functools.partial(_diff_attn_kernel, kc=kc, lambda_init=lambda_init)
    return pl.pallas_call(
        kern,
        grid=(n_heads, nq // bq),
        in_specs=in_specs,
        out_specs=pl.BlockSpec((bq, DIFF_V_DIM), lambda h, i: (i, h)),
        out_shape=jax.ShapeDtypeStruct((nq, d), BF16),
        scratch_shapes=[pltpu.VMEM((2, bq, kc), F32), pltpu.VMEM((2, bq, kc), F32),
                        pltpu.VMEM((bq, 1), F32), pltpu.VMEM((bq, 1), F32), pltpu.VMEM((bq, DIFF_V_DIM), F32),
                        pltpu.VMEM((bq, 1), F32), pltpu.VMEM((bq, 1), F32), pltpu.VMEM((bq, DIFF_V_DIM), F32)],
        compiler_params=_params("arbitrary", "arbitrary"),
        name="diff_attn",
    )(*args)


def _retention_kernel(dec_ref, q_ref, k_ref, v_ref, g_ref, kc_ref, vc_ref, *rest, backward, has_prev):
    if has_prev:
        prev_ref, o_ref, s_ref, intra_ref, qd_ref, kd_ref, cd_ref = rest
    else:
        prev_ref = None
        o_ref, s_ref, intra_ref, qd_ref, kd_ref, cd_ref = rest
    c = RET_CHUNK
    n_chunks = q_ref.shape[0] // c
    dn_t = (((0,), (0,)), ((), ()))
    dn_nt = (((1,), (1,)), ((), ()))

    @pl.when(pl.program_id(1) == 0)
    def _():
        lg = jnp.log1p(-jnp.exp(dec_ref[0]))[0:1, 0:1]
        i = lax.broadcasted_iota(I32, (c, c), 0)
        j = lax.broadcasted_iota(I32, (c, c), 1)
        pos = lax.broadcasted_iota(I32, (c, 1), 0).astype(F32)
        rel = (j - i) if backward else (i - j)
        relf = jnp.maximum(rel, 0).astype(F32)
        intra_ref[...] = jnp.where(rel >= 0, jnp.exp(lg * relf), 0.0)
        if backward:
            qd_ref[...] = jnp.exp(lg * (c - pos))
            kd_ref[...] = jnp.exp(lg * pos)
        else:
            qd_ref[...] = jnp.exp(lg * (pos + 1.0))
            kd_ref[...] = jnp.exp(lg * (c - 1.0 - pos))
        cd_ref[...] = jnp.broadcast_to(jnp.exp(lg * float(c)), cd_ref.shape)
        kd = (kc_ref[...].astype(F32) * kd_ref[...]).astype(BF16)
        s_ref[...] = lax.dot_general(kd, vc_ref[...], dn_t, preferred_element_type=F32)

    def body(t, carry):
        cc = (n_chunks - 1 - t) if backward else t
        rows = pl.ds(pl.multiple_of(cc * c, c), c)
        q, k, v = q_ref[rows, :], k_ref[rows, :], v_ref[rows, :]
        scores = lax.dot_general(q, k, dn_nt, preferred_element_type=F32) * intra_ref[...]
        s_old = s_ref[...]
        qd = (q.astype(F32) * qd_ref[...]).astype(BF16)
        o = (jnp.dot(scores.astype(BF16), v, preferred_element_type=F32)
             + jnp.dot(qd, s_old.astype(BF16), preferred_element_type=F32))
        kd = (k.astype(F32) * kd_ref[...]).astype(BF16)
        s_ref[...] = s_old * cd_ref[0:1, 0:1] + lax.dot_general(kd, v, dn_t, preferred_element_type=F32)
        out = _silu(g_ref[rows, :].astype(F32)) * _rms(o)
        if has_prev:
            out = out + prev_ref[rows, :].astype(F32)
        o_ref[rows, :] = out.astype(o_ref.dtype)
        return carry

    lax.fori_loop(0, n_chunks, body, 0, unroll=2)


def retention_dir(proj, decay, n_heads, t_lat, backward, prev=None):
    c = RET_CHUNK
    d = n_heads * RET_QK_DIM
    t_ctx = proj.shape[0] - t_lat
    assert t_ctx == c and t_lat % c == 0
    rows = _largest_divisor(t_lat, (2048, 1024, 512, 256))
    n_steps = t_lat // rows
    step = (lambda s: n_steps - 1 - s) if backward else (lambda s: s)
    qb, vb = d // RET_QK_DIM, 2 * d // RET_V_DIM
    gb = (6 if backward else 4) * d // RET_V_DIM
    ctx_blk = t_lat // c
    in_specs = [pl.BlockSpec((1, 8, LANES), lambda h, s: (h, 0, 0)),
                pl.BlockSpec((rows, RET_QK_DIM), lambda h, s: (step(s), h)),
                pl.BlockSpec((rows, RET_QK_DIM), lambda h, s: (step(s), qb + h)),
                pl.BlockSpec((rows, RET_V_DIM), lambda h, s: (step(s), vb + h)),
                pl.BlockSpec((rows, RET_V_DIM), lambda h, s: (step(s), gb + h)),
                pl.BlockSpec((c, RET_QK_DIM), lambda h, s: (ctx_blk, qb + h)),
                pl.BlockSpec((c, RET_V_DIM), lambda h, s: (ctx_blk, vb + h))]
    args = [jnp.broadcast_to(decay.astype(F32)[:, None, None], (n_heads, 8, LANES)), proj, proj, proj, proj, proj, proj]
    if prev is not None:
        in_specs.append(pl.BlockSpec((rows, RET_V_DIM), lambda h, s: (step(s), h)))
        args.append(prev)
    kern = functools.partial(_retention_kernel, backward=backward, has_prev=prev is not None)
    return pl.pallas_call(
        kern,
        grid=(n_heads, n_steps),
        in_specs=in_specs,
        out_specs=pl.BlockSpec((rows, RET_V_DIM), lambda h, s: (step(s), h)),
        out_shape=jax.ShapeDtypeStruct((t_lat, n_heads * RET_V_DIM), BF16),
        scratch_shapes=[pltpu.VMEM((RET_QK_DIM, RET_V_DIM), F32), pltpu.VMEM((c, c), F32),
                        pltpu.VMEM((c, 1), F32), pltpu.VMEM((c, 1), F32), pltpu.VMEM((8, LANES), F32)],
        compiler_params=_params("arbitrary", "arbitrary"),
        name="retention_bwd" if backward else "retention_fwd",
    )(*args)


def _norm_logits_kernel(h_ref, gain_ref, sh_ref, sc_ref, wr_ref, lg_ref, *, n_lat_tiles):
    is_ctx = pl.program_id(0) >= n_lat_tiles
    y = _rms(h_ref[...]) * gain_ref[...]
    n2 = (y * (1.0 + _mod_rows(sc_ref, is_ctx)) + _mod_rows(sh_ref, is_ctx)).astype(BF16)
    lg_ref[...] = jnp.dot(n2, wr_ref[...].astype(BF16), preferred_element_type=F32)


def norm_logits(h, gain, mod, k_shift, k_scale, w_router, li, t_lat):
    rows, d = h.shape
    e = w_router.shape[2]
    return pl.pallas_call(
        functools.partial(_norm_logits_kernel, n_lat_tiles=t_lat // ROW_TILE),
        grid=(rows // ROW_TILE,),
        in_specs=[pl.BlockSpec((ROW_TILE, d), lambda i: (i, 0)),
                  pl.BlockSpec((1, d), lambda i: (0, 0)),
                  pl.BlockSpec((8, d), lambda i: (0, k_shift)),
                  pl.BlockSpec((8, d), lambda i: (0, k_scale)),
                  pl.BlockSpec((None, d, e), lambda i: (li, 0, 0))],
        out_specs=pl.BlockSpec((ROW_TILE, e), lambda i: (i, 0)),
        out_shape=jax.ShapeDtypeStruct((rows, e), F32),
        compiler_params=_params("arbitrary"),
        name="norm_logits",
    )(h, gain.reshape(1, d), mod, mod, w_router)


def _route_kernel(lg_ref, idx_ref, gate_ref, bits_ref, key_ref, w_ref, *, cap):
    t, e = lg_ref.shape
    pb = min(t, 256)
    tb = min(t, 1024)
    lg = lg_ref[...]
    ex = jnp.exp(lg - jnp.max(lg, axis=-1, keepdims=True))
    aff = ex / jnp.sum(ex, axis=-1, keepdims=True)
    bits_ref[...] = lax.bitcast_convert_type(aff, I32)
    ahi = aff.astype(BF16)
    r1 = aff - ahi.astype(F32)
    amid = r1.astype(BF16)
    alo = (r1 - amid.astype(F32)).astype(BF16)
    tok = lax.broadcasted_iota(I32, (t, e), 0)
    lane = lax.broadcasted_iota(I32, (t, e), 1)
    tdig = jnp.where(lane == 0, tok >> 7, jnp.where(lane == 1, tok & 127, 0)).astype(F32).astype(BF16)
    pr = lax.broadcasted_iota(I32, (e, 4 * e), 0)
    pc = lax.broadcasted_iota(I32, (e, 4 * e), 1)
    packed = jnp.zeros((t, 4 * e), F32)
    for piece, part in enumerate((ahi, amid, alo, tdig)):
        place = jnp.where(pc == pr + piece * e, 1.0, 0.0).astype(BF16)
        packed = packed + jnp.dot(part, place, preferred_element_type=F32)
    w_ref[...] = packed.astype(BF16)

    def thr_body(it, thr):
        cand = thr | (jnp.int32(1) << (30 - it))
        cnt = jnp.sum(jnp.where(bits_ref[...] >= cand, 1.0, 0.0), axis=0, keepdims=True)
        return jnp.where(cnt >= cap, cand, thr)

    thr = lax.fori_loop(0, 31, thr_body, jnp.zeros((1, e), I32))
    n_gt = jnp.sum(jnp.where(bits_ref[...] > thr, 1.0, 0.0), axis=0, keepdims=True)
    need = cap - n_gt

    ri = lax.broadcasted_iota(I32, (pb, pb), 0)
    ci = lax.broadcasted_iota(I32, (pb, pb), 1)
    ltri = jnp.where(ci < ri, 1.0, 0.0).astype(BF16)

    def pre_body(b, carry):
        ceq, csel = carry
        rows = pl.ds(pl.multiple_of(b * pb, pb), pb)
        bt = bits_ref[rows, :]
        gtb = bt > thr
        eqf = jnp.where(bt == thr, 1.0, 0.0)
        eq_rank = jnp.dot(ltri, eqf.astype(BF16), preferred_element_type=F32) + ceq
        self_ = jnp.where(gtb, 1.0, jnp.where(eq_rank < need, eqf, 0.0))
        pos = jnp.dot(ltri, self_.astype(BF16), preferred_element_type=F32) + csel
        key_ref[rows, :] = jnp.where(self_ > 0.0, pos, -1.0).astype(I32)
        return (ceq + jnp.sum(eqf, axis=0, keepdims=True), csel + jnp.sum(self_, axis=0, keepdims=True))

    zero = jnp.zeros((1, e), F32)
    lax.fori_loop(0, t // pb, pre_body, (zero, zero))

    slot = lax.broadcasted_iota(I32, (1, cap), 1)
    dn_t = (((0,), (0,)), ((), ()))
    for ex_i in range(e):
        def slot_body(b, carry, ex_i=ex_i):
            rows = pl.ds(pl.multiple_of(b * tb, tb), tb)
            onehot = jnp.where(key_ref[rows, ex_i:ex_i + 1] == slot, 1.0, 0.0).astype(BF16)
            return carry + lax.dot_general(w_ref[rows, :], onehot, dn_t, preferred_element_type=F32)

        acc = lax.fori_loop(0, t // tb, slot_body, jnp.zeros((4 * e, cap), F32))
        gate_ref[ex_i:ex_i + 1, :] = ((acc[ex_i:ex_i + 1, :] + acc[e + ex_i:e + ex_i + 1, :])
                                      + acc[2 * e + ex_i:2 * e + ex_i + 1, :])
        idx_ref[ex_i:ex_i + 1, :] = (acc[3 * e:3 * e + 1, :] * 128.0 + acc[3 * e + 1:3 * e + 2, :]).astype(I32)


def route(logits, cap):
    t, e = logits.shape
    return pl.pallas_call(
        functools.partial(_route_kernel, cap=cap),
        out_shape=(jax.ShapeDtypeStruct((e, cap), I32), jax.ShapeDtypeStruct((e, cap), F32)),
        scratch_shapes=[pltpu.VMEM((t, e), I32), pltpu.VMEM((t, e), I32), pltpu.VMEM((t, 4 * e), BF16)],
        compiler_params=pltpu.CompilerParams(vmem_limit_bytes=VMEM_LIMIT),
        name="route",
    )(logits)


def _issue_rows(n, fn):
    lax.fori_loop(0, n, lambda r, carry: (fn(r), carry)[1], 0, unroll=DMA_UNROLL)


def _gather_kernel(idx_sm, h_hbm, gain_ref, sh_ref, sc_ref, o_ref, buf, rinv_ref, mul_ref, sh_b_ref, sems,
                   *, sb, n_lat_slots):
    base = pl.program_id(0) * buf.shape[0]
    n_sub = buf.shape[0] // sb

    def row_copy(r, tok, j):
        return pltpu.make_async_copy(h_hbm.at[pl.ds(tok, 1), :], buf.at[pl.ds(r, 1), :], sems.at[j])

    def start_sub(j, carry):
        _issue_rows(sb, lambda r: row_copy(j * sb + r, idx_sm[base + j * sb + r], j).start())
        return carry

    lax.fori_loop(0, n_sub, start_sub, 0)
    for kind in range(2):
        mul_ref[kind] = jnp.broadcast_to(gain_ref[...] * (1.0 + sc_ref[kind:kind + 1, :]), mul_ref.shape[1:])
        sh_b_ref[kind] = jnp.broadcast_to(sh_ref[kind:kind + 1, :], sh_b_ref.shape[1:])

    def finish_sub(j, carry):
        _issue_rows(sb, lambda r: row_copy(j * sb + r, 0, j).wait())
        xs = buf[pl.ds(pl.multiple_of(j * sb, sb), sb), :]
        rinv_ref[...] = lax.rsqrt(jnp.mean(xs * xs, axis=-1, keepdims=True) + NORM_EPS)

        def group(g, c2):
            r0 = pl.multiple_of(j * sb + g * ROW_GROUP, ROW_GROUP)
            rows = pl.ds(r0, ROW_GROUP)
            kind = (r0 >= n_lat_slots).astype(I32)
            y = buf[rows, :] * rinv_ref[pl.ds(pl.multiple_of(g * ROW_GROUP, ROW_GROUP), ROW_GROUP), :]
            o_ref[rows, :] = (y * mul_ref[kind] + sh_b_ref[kind]).astype(o_ref.dtype)
            return c2

        lax.fori_loop(0, sb // ROW_GROUP, group, 0, unroll=True)
        return carry

    lax.fori_loop(0, n_sub, finish_sub, 0)


def moe_gather(h, idx, gain, mod, k_shift, k_scale, n_lat_slots, sb):
    e, ct = idx.shape
    d = h.shape[1]
    assert n_lat_slots % ROW_GROUP == 0
    grid_spec = pltpu.PrefetchScalarGridSpec(
        num_scalar_prefetch=1,
        grid=(e,),
        in_specs=[pl.BlockSpec(memory_space=pl.ANY),
                  pl.BlockSpec((1, d), lambda i, idx_sm: (0, 0)),
                  pl.BlockSpec((8, d), lambda i, idx_sm: (0, k_shift)),
                  pl.BlockSpec((8, d), lambda i, idx_sm: (0, k_scale))],
        out_specs=pl.BlockSpec((None, ct, d), lambda i, idx_sm: (i, 0, 0)),
        scratch_shapes=[pltpu.VMEM((ct, d), F32), pltpu.VMEM((sb, 1), F32),
                        pltpu.VMEM((2, ROW_GROUP, d), F32), pltpu.VMEM((2, ROW_GROUP, d), F32),
                        pltpu.SemaphoreType.DMA((ct // sb,))],
    )
    return pl.pallas_call(
        functools.partial(_gather_kernel, sb=sb, n_lat_slots=n_lat_slots),
        grid_spec=grid_spec,
        out_shape=jax.ShapeDtypeStruct((e, ct, d), BF16),
        compiler_params=_params("arbitrary"),
        name="moe_gather",
    )(idx.reshape(-1), h, gain.reshape(1, d), mod, mod)


def _ffn_up_kernel(x_ref, wg_ref, wu_ref, o_ref):
    x = x_ref[...]
    g = jnp.dot(x, wg_ref[...].astype(BF16), preferred_element_type=F32)
    u = jnp.dot(x, wu_ref[...].astype(BF16), preferred_element_type=F32)
    o_ref[...] = (_silu(g) * u).astype(o_ref.dtype)


def ffn_up(xg, w_gate, w_up, li):
    e, ct, d = xg.shape
    f = w_gate.shape[3]
    fb = 256
    return pl.pallas_call(
        _ffn_up_kernel,
        grid=(e, f // fb),
        in_specs=[pl.BlockSpec((None, ct, d), lambda i, j: (i, 0, 0)),
                  pl.BlockSpec((None, None, d, fb), lambda i, j: (li, i, 0, j)),
                  pl.BlockSpec((None, None, d, fb), lambda i, j: (li, i, 0, j))],
        out_specs=pl.BlockSpec((None, ct, fb), lambda i, j: (i, 0, j)),
        out_shape=jax.ShapeDtypeStruct((e, ct, f), BF16),
        compiler_params=_params("arbitrary", "arbitrary"),
        name="ffn_up",
    )(xg, w_gate, w_up)


def _ffn_down_kernel(h_ref, wd_ref, gate_ref, g2_ref, o_ref, *, n_lat_slots):
    acc = jnp.dot(h_ref[...], wd_ref[...].astype(BF16), preferred_element_type=F32)
    is_ctx = lax.broadcasted_iota(I32, (acc.shape[0], 1), 0) >= n_lat_slots
    g2 = jnp.where(is_ctx, g2_ref[1:2, :], g2_ref[0:1, :])
    o_ref[...] = (acc * gate_ref[...] * g2).astype(o_ref.dtype)


def ffn_down(hid, w_down, li, gate, mod, k_gate, n_lat_slots):
    e, ct, f = hid.shape
    d = w_down.shape[3]
    nb = _largest_divisor(d, (2048, 1024, 512, 256, 128))
    nblk = d // nb
    return pl.pallas_call(
        functools.partial(_ffn_down_kernel, n_lat_slots=n_lat_slots),
        grid=(e, nblk),
        in_specs=[pl.BlockSpec((None, ct, f), lambda i, j: (i, 0, 0)),
                  pl.BlockSpec((None, None, f, nb), lambda i, j: (li, i, 0, j)),
                  pl.BlockSpec((None, ct, 1), lambda i, j: (i, 0, 0)),
                  pl.BlockSpec((8, nb), lambda i, j: (0, k_gate * nblk + j))],
        out_specs=pl.BlockSpec((None, ct, nb), lambda i, j: (i, 0, j)),
        out_shape=jax.ShapeDtypeStruct((e, ct, d), BF16),
        compiler_params=_params("arbitrary", "arbitrary"),
        name="ffn_down",
    )(hid, w_down, gate[:, :, None], mod)


def _scatter_kernel(idx_sm, y_ref, acc_in, acc_out, buf, sems_in, sem_out, *, sb):
    del acc_in
    ct = buf.shape[0]
    base = pl.program_id(0) * ct
    n_sub = ct // sb

    def fetch(r, tok, j):
        return pltpu.make_async_copy(acc_out.at[pl.ds(tok, 1), :], buf.at[pl.ds(r, 1), :], sems_in.at[j])

    def put(r, tok):
        return pltpu.make_async_copy(buf.at[pl.ds(r, 1), :], acc_out.at[pl.ds(tok, 1), :], sem_out)

    def start_sub(j, carry):
        _issue_rows(sb, lambda r: fetch(j * sb + r, idx_sm[base + j * sb + r], j).start())
        return carry

    lax.fori_loop(0, n_sub, start_sub, 0)

    def finish_sub(j, carry):
        _issue_rows(sb, lambda r: fetch(j * sb + r, 0, j).wait())
        rows = pl.ds(pl.multiple_of(j * sb, sb), sb)
        buf[rows, :] = buf[rows, :] + y_ref[rows, :].astype(F32)
        _issue_rows(sb, lambda r: put(j * sb + r, idx_sm[base + j * sb + r]).start())
        return carry

    lax.fori_loop(0, n_sub, finish_sub, 0)
    _issue_rows(ct, lambda r: put(r, 0).wait())


def moe_scatter_add(acc, idx, y, sb):
    e, ct = idx.shape
    rows, d = acc.shape
    grid_spec = pltpu.PrefetchScalarGridSpec(
        num_scalar_prefetch=1,
        grid=(e,),
        in_specs=[pl.BlockSpec((None, ct, d), lambda i, idx_sm: (i, 0, 0)),
                  pl.BlockSpec(memory_space=pl.ANY)],
        out_specs=pl.BlockSpec(memory_space=pl.ANY),
        scratch_shapes=[pltpu.VMEM((ct, d), F32), pltpu.SemaphoreType.DMA((ct // sb,)),
                        pltpu.SemaphoreType.DMA(())],
    )
    return pl.pallas_call(
        functools.partial(_scatter_kernel, sb=sb),
        grid_spec=grid_spec,
        out_shape=jax.ShapeDtypeStruct((rows, d), F32),
        input_output_aliases={2: 0},
        compiler_params=_params("arbitrary"),
        name="moe_scatter_add",
    )(idx.reshape(-1), y, acc)


def expert_choice_moe(h, gain, mod, w_router, w_gate, w_up, w_down, li, groups, t_lat):
    e = w_router.shape[2]
    logits = norm_logits(h, gain, mod, 3, 4, w_router, li, t_lat)
    idx_parts, gate_parts = [], []
    for start, count in groups:
        idx_g, gate_g = route(logits[start:start + count], EC_CAPACITY_FACTOR * count // e)
        idx_parts.append(idx_g + start)
        gate_parts.append(gate_g)
    idx = jnp.concatenate(idx_parts, axis=1)
    gate = jnp.concatenate(gate_parts, axis=1)
    ct = idx.shape[1]
    sb = _largest_divisor(ct, (128, 96, 64, 32, 16))
    n_lat_slots = EC_CAPACITY_FACTOR * t_lat // e
    xg = moe_gather(h, idx, gain, mod, 3, 4, n_lat_slots, sb)
    hid = ffn_up(xg, w_gate, w_up, li)
    y = ffn_down(hid, w_down, li, gate, mod, 5, n_lat_slots)
    return moe_scatter_add(h, idx, y, sb)


def _axial_tables(t_lat, t_ctx, rot_dim):
    axis_dim = rot_dim // 2
    tok = jnp.arange(t_lat)
    row = (tok // GRID_W).astype(F32)
    col = (tok % GRID_W).astype(F32)
    inv_freq = ROPE_BASE ** (-jnp.arange(0, axis_dim, 2, dtype=F32) / axis_dim)
    ar, ac = row[:, None] * inv_freq, col[:, None] * inv_freq
    cos = jnp.concatenate([jnp.cos(ar), jnp.cos(ar), jnp.cos(ac), jnp.cos(ac)], axis=-1)
    sin = jnp.concatenate([-jnp.sin(ar), jnp.sin(ar), -jnp.sin(ac), jnp.sin(ac)], axis=-1)
    cos = jnp.concatenate([cos, jnp.ones((t_ctx, rot_dim), F32)], axis=0)
    sin = jnp.concatenate([sin, jnp.zeros((t_ctx, rot_dim), F32)], axis=0)
    return cos, sin


def kernel(x, c, ctx, c_ctx, ada_w, ada_b, norm_mix, norm_ffn, diff_w_in, diff_w_out, diff_q_norm, diff_k_norm, diff_lambda_q1, diff_lambda_k1, diff_lambda_q2, diff_lambda_k2, diff_subln, ret_w_in, ret_w_out, ret_decay_fwd, ret_decay_bwd, router_w, expert_w_gate, expert_w_up, expert_w_down):
    batch, t_lat, d = x.shape
    t_ctx = ctx.shape[1]
    depth = ada_w.shape[0]
    assert batch == 1 and depth == 2 and t_lat % ROW_TILE == 0 and t_ctx == ROW_TILE
    t_all = t_lat + t_ctx
    diff_heads = d // DIFF_V_DIM
    ret_heads = d // RET_QK_DIM

    h = jnp.concatenate([x[0], ctx[0]], axis=0)
    cvec = jnp.zeros((8, d), F32).at[0].set(c[0]).at[1].set(c_ctx)
    cos_d, sin_d = _axial_tables(t_lat, t_ctx, DIFF_HEAD_DIM)
    cos_r, sin_r = _axial_tables(t_lat, t_ctx, RET_QK_DIM)

    mod = adaln(cvec, ada_w, ada_b, 0)
    n1 = norm_mod(h, norm_mix[0], mod, 0, 1, t_lat)
    qkv = mm_diff_in(n1, diff_w_in, 0, diff_q_norm[0], diff_k_norm[0], cos_d, sin_d)
    lam_vecs = (diff_lambda_q1[0], diff_lambda_k1[0], diff_lambda_q2[0], diff_lambda_k2[0])
    lambda_init = 0.8 - 0.6 * math.exp(-0.3 * 0)
    bq = _largest_divisor(t_lat, (512, 256))
    kc = _largest_divisor(t_all, (2816, 1408, 768, 256))
    att_lat = diff_attn(qkv, lam_vecs, diff_subln[0], diff_heads, (0, t_lat), (0, t_all), bq, kc, lambda_init)
    att_ctx = diff_attn(qkv, lam_vecs, diff_subln[0], diff_heads, (t_lat, t_ctx), (t_lat, t_ctx), t_ctx, t_ctx,
                        lambda_init)
    h = mm_resid(jnp.concatenate([att_lat, att_ctx], axis=0), diff_w_out, 0, h, mod, 2, t_lat)
    h = expert_choice_moe(h, norm_ffn[0], mod, router_w, expert_w_gate, expert_w_up, expert_w_down, 0,
                          [(0, t_lat), (t_lat, t_ctx)], t_lat)

    mod = adaln(cvec, ada_w, ada_b, 1)
    n1 = norm_mod(h, norm_mix[1], mod, 0, 1, t_lat)
    proj = mm_ret_in(n1, ret_w_in, 0, cos_r, sin_r)
    gated = retention_dir(proj, ret_decay_fwd[0], ret_heads, t_lat, backward=False)
    gated = retention_dir(proj, ret_decay_bwd[0], ret_heads, t_lat, backward=True, prev=gated)
    h = mm_resid(gated, ret_w_out, 0, h, mod, 2, t_lat, wide_cols=True)
    h = expert_choice_moe(h, norm_ffn[1], mod, router_w, expert_w_gate, expert_w_up, expert_w_down, 1,
                          [(0, t_lat)], t_lat)
    return h[None]
```
